```python
import math
import jax
import jax.numpy as jnp
from jax import lax
import numpy as np

D_MODEL = 4096
BATCH = 2
SEQ = 8192
DEPTH = 4

HEAD_DIM = 128
ATT_PATTERNS = ((128, 1), (512, 4), (2048, 16))
N_ATT_GROUPS = 3
ATT_HEADS_PER_GROUP = 4
N_ATT_HEADS = N_ATT_GROUPS * ATT_HEADS_PER_GROUP
ATT_WIDTH = N_ATT_HEADS * HEAD_DIM
ATT_OUT_WIDTH = ATT_HEADS_PER_GROUP * HEAD_DIM
BAND_BLOCK = 128
LRU_WIDTH = 3 * D_MODEL // 8
LRU_BLOCKS = 12
LRU_BLOCK_SIZE = LRU_WIDTH // LRU_BLOCKS
LRU_C = 8.0
CONV_WIDTH = 4
SSD_D_INNER = 3 * D_MODEL // 8
SSD_HEAD_DIM = 64
SSD_HEADS = SSD_D_INNER // SSD_HEAD_DIM
SSD_GROUPS = 4
SSD_STATE = 128
SSD_CHUNK = 128
SSD_CONV_DIM = SSD_D_INNER + 2 * SSD_GROUPS * SSD_STATE
N_BRANCHES = 3
GATE_RANK = 512
IN_SPLIT_SIZES = (ATT_WIDTH, ATT_WIDTH, ATT_WIDTH, LRU_WIDTH, LRU_WIDTH, SSD_D_INNER, SSD_CONV_DIM, SSD_HEADS, GATE_RANK)
IN_WIDTH = 3 * ATT_WIDTH + 2 * LRU_WIDTH + SSD_D_INNER + SSD_CONV_DIM + SSD_HEADS + GATE_RANK
FFN_HIDDEN = D_MODEL
MEM_LEN = 256
XATTN_HEADS = 4
XATTN_HEAD_DIM = 128
NORM_EPS = 1e-6

kernel_name = "hybrid_gated_dilated_attn_rglru_ssd_block"


def split_cols(t, sizes):
    out, off = [], 0
    for size in sizes:
        out.append(t[..., off:off + size])
        off += size
    return out


def rms_norm(x, g):
    xf = x.astype(jnp.float32)
    y = xf * lax.rsqrt(jnp.mean(xf * xf, axis=-1, keepdims=True) + NORM_EPS)
    return (y * g.astype(jnp.float32)).astype(x.dtype)


def swiglu(x, w_gu, w_down):
    gate, up = jnp.split(x @ w_gu, 2, axis=-1)
    return (jax.nn.silu(gate) * up) @ w_down


def causal_depthwise_conv(x, w, b):
    k = w.shape[0]
    y = lax.conv_general_dilated(
        x, w[:, None, :].astype(x.dtype), window_strides=(1,), padding=[(k - 1, 0)],
        dimension_numbers=("NWC", "WIO", "NWC"), feature_group_count=x.shape[-1])
    return y + b


def alibi_slopes():
    h = jnp.arange(1, N_ATT_HEADS + 1, dtype=jnp.float32)
    return jnp.exp2(-8.0 * h / N_ATT_HEADS)


def dilated_causal_attention(q, k, v, window, dilation, slopes):
    bsz, s, nh, e = q.shape
    win_steps = window // dilation
    sub_len = s // dilation
    nb = -(-sub_len // BAND_BLOCK)
    pad_len = nb * BAND_BLOCK - sub_len

    def to_sub(t):
        t = t.reshape(bsz, sub_len, dilation, nh, e).transpose(0, 2, 1, 3, 4)
        t = jnp.pad(t, ((0, 0), (0, 0), (0, pad_len), (0, 0), (0, 0)))
        return t.reshape(bsz, dilation, nb, BAND_BLOCK, nh, e)

    def band(t):
        prev = jnp.pad(t, ((0, 0), (0, 0), (1, 0), (0, 0), (0, 0), (0, 0)))[:, :, :-1]
        return jnp.concatenate([prev, t], axis=3)

    qb = to_sub(q)
    kb = band(to_sub(k))
    vb = band(to_sub(v))
    scores = jnp.einsum("brnqhe,brnkhe->brnhqk", qb, kb,
                        preferred_element_type=jnp.float32) * (e ** -0.5)
    qi = jnp.arange(BAND_BLOCK)[:, None]
    kj = jnp.arange(2 * BAND_BLOCK)[None, :]
    lag = qi + BAND_BLOCK - kj
    first = (jnp.arange(nb) == 0)[:, None, None]
    valid = (lag >= 0) & (lag <= win_steps) & ~(first & (kj < BAND_BLOCK))
    alibi = -slopes.astype(jnp.float32)[:, None, None] * (lag * dilation).astype(jnp.float32)
    scores = jnp.where(valid[:, None], scores + alibi, -jnp.inf)
    lse = jax.nn.logsumexp(scores, axis=-1)
    probs = jnp.exp(scores - lse[..., None])
    out = jnp.einsum("brnhqk,brnkhe->brnqhe", probs.astype(v.dtype), vb)
    out = out.reshape(bsz, dilation, nb * BAND_BLOCK, nh, e)[:, :, :sub_len]
    out = out.transpose(0, 2, 1, 3, 4).reshape(bsz, s, nh, e)
    lse = lse.transpose(0, 1, 2, 4, 3).reshape(bsz, dilation, nb * BAND_BLOCK, nh)[:, :, :sub_len]
    lse = lse.transpose(0, 2, 1, 3).reshape(bsz, s, nh)
    return out, lse


def _lru_combine(left, right):
    a_l, b_l = left
    a_r, b_r = right
    return a_l * a_r, a_r * b_l + b_r


def rg_lru(x, w_a, b_a, w_x, b_x, lam):
    bsz, s, w = x.shape
    f32 = jnp.float32
    xf = x.astype(f32)
    xb = xf.reshape(bsz, s, LRU_BLOCKS, LRU_BLOCK_SIZE)
    r = jax.nn.sigmoid(jnp.einsum("bsnc,ncd->bsnd", xb, w_a.astype(f32)).reshape(bsz, s, w) + b_a.astype(f32))
    i = jax.nn.sigmoid(jnp.einsum("bsnc,ncd->bsnd", xb, w_x.astype(f32)).reshape(bsz, s, w) + b_x.astype(f32))
    log_a = -LRU_C * r * jax.nn.softplus(-lam.astype(f32))
    a = jnp.exp(log_a)
    b = jnp.sqrt(-jnp.expm1(2.0 * log_a)) * (i * xf)
    _, h = lax.associative_scan(_lru_combine, (a, b), axis=1)
    return h.astype(x.dtype)


def segsum(x):
    t = x.shape[-1]
    xr = jnp.broadcast_to(x[..., None], x.shape + (t,))
    xr = jnp.where(jnp.tril(jnp.ones((t, t), bool), -1), xr, 0.0)
    cs = jnp.cumsum(xr, axis=-2)
    return jnp.where(jnp.tril(jnp.ones((t, t), bool)), cs, -jnp.inf)


def ssd_chunked(x, dt, a, bm, cm):
    bsz, s, _, p = x.shape
    g, n = bm.shape[2], bm.shape[3]
    e = SSD_HEADS // SSD_GROUPS
    nc = s // SSD_CHUNK
    lc = SSD_CHUNK
    dt_c = dt.reshape(bsz, nc, lc, g, e)
    xdt = x.reshape(bsz, nc, lc, g, e, p) * dt_c[..., None]
    bc = bm.reshape(bsz, nc, lc, g, n)
    cc = cm.reshape(bsz, nc, lc, g, n)
    da = (dt_c * a.reshape(g, e)).transpose(0, 3, 4, 1, 2)
    da_cs = jnp.cumsum(da, axis=-1)
    decay_in = jnp.exp(segsum(da))
    cb = jnp.einsum("bclgn,bcsgn->bgcls", cc, bc)
    y_diag = jnp.einsum("bgecls,bcsgep->bclgep", cb[:, :, None] * decay_in, xdt)
    decay_to_end = jnp.exp(da_cs[..., -1:] - da_cs).transpose(0, 3, 4, 1, 2)
    states = jnp.einsum("bcsgn,bcsgep->bcgepn", bc, xdt * decay_to_end[..., None])
    states = jnp.concatenate([jnp.zeros_like(states[:, :1]), states], axis=1)
    chunk_decay = jnp.exp(segsum(jnp.pad(da_cs[..., -1], ((0, 0), (0, 0), (0, 0), (1, 0)))))
    states = jnp.einsum("bgezc,bcgepn->bzgepn", chunk_decay, states)[:, :-1]
    decay_from_start = jnp.exp(da_cs).transpose(0, 3, 4, 1, 2)
    y_off = jnp.einsum("bclgn,bcgepn->bclgep", cc, states) * decay_from_start[..., None]
    return (y_diag + y_off).reshape(bsz, s, SSD_HEADS, p)


def mamba2_ssd(z, xbc, dt, conv_w, conv_b, dt_bias, a_log, d_skip, norm_g):
    bsz, s, _ = z.shape
    f32 = jnp.float32
    xbc = jax.nn.silu(causal_depthwise_conv(xbc, conv_w, conv_b))
    xs, bm, cm = split_cols(xbc, (SSD_D_INNER, SSD_GROUPS * SSD_STATE, SSD_GROUPS * SSD_STATE))
    dt = jax.nn.softplus(dt.astype(f32) + dt_bias.astype(f32))
    a = -jnp.exp(a_log.astype(f32))
    xh = xs.astype(f32).reshape(bsz, s, SSD_HEADS, SSD_HEAD_DIM)
    y = ssd_chunked(xh, dt, a,
                    bm.astype(f32).reshape(bsz, s, SSD_GROUPS, SSD_STATE),
                    cm.astype(f32).reshape(bsz, s, SSD_GROUPS, SSD_STATE))
    y = y + xh * d_skip.astype(f32)[:, None]
    y = y.reshape(bsz, s, SSD_D_INNER) * jax.nn.silu(z.astype(f32))
    yg = y.reshape(bsz, s, SSD_GROUPS, SSD_D_INNER // SSD_GROUPS)
    yg = yg * lax.rsqrt(jnp.mean(yg * yg, axis=-1, keepdims=True) + NORM_EPS)
    return (yg.reshape(bsz, s, SSD_D_INNER) * norm_g.astype(f32)).astype(z.dtype)


def hybrid_mixer(u, w_in, lru_conv_w, lru_conv_b, lru_w_a, lru_b_a, lru_w_x, lru_b_x, lru_lambda,
                 ssd_conv_w, ssd_conv_b, ssd_dt_bias, ssd_a_log, ssd_d, ssd_norm,
                 gate_up, gate_b, w_br_attn, w_br_lru, w_br_ssd, w_out):
    bsz, s, _ = u.shape
    d = D_MODEL
    q, k, v, lru_x, lru_g, ssd_z, ssd_xbc, ssd_dt, g_low = split_cols(u @ w_in, IN_SPLIT_SIZES)
    shp = (bsz, s, N_ATT_GROUPS, ATT_HEADS_PER_GROUP, HEAD_DIM)
    q, k, v = q.reshape(shp), k.reshape(shp), v.reshape(shp)
    slopes = alibi_slopes()
    outs, lses = [], []
    for gi, (window, dilation) in enumerate(ATT_PATTERNS):
        hs = slice(gi * ATT_HEADS_PER_GROUP, (gi + 1) * ATT_HEADS_PER_GROUP)
        o, lse = dilated_causal_attention(q[:, :, gi], k[:, :, gi], v[:, :, gi], window, dilation, slopes[hs])
        outs.append(o)
        lses.append(lse)
    mix = jax.nn.softmax(jnp.stack(lses), axis=0)
    y_attn = jnp.einsum("gbsh,gbshe->bshe", mix, jnp.stack(outs).astype(jnp.float32))
    y_attn = y_attn.astype(u.dtype).reshape(bsz, s, ATT_OUT_WIDTH)
    y_lru = jax.nn.gelu(lru_g) * rg_lru(causal_depthwise_conv(lru_x, lru_conv_w, lru_conv_b),
                                        lru_w_a, lru_b_a, lru_w_x, lru_b_x, lru_lambda)
    y_ssd = mamba2_ssd(ssd_z, ssd_xbc, ssd_dt, ssd_conv_w, ssd_conv_b, ssd_dt_bias, ssd_a_log, ssd_d, ssd_norm)
    y = (jax.nn.sigmoid(g_low @ gate_up[:, :d] + gate_b[:d]) * (y_attn @ w_br_attn)
         + jax.nn.sigmoid(g_low @ gate_up[:, d:2 * d] + gate_b[d:2 * d]) * (y_lru @ w_br_lru)
         + jax.nn.sigmoid(g_low @ gate_up[:, 2 * d:] + gate_b[2 * d:]) * (y_ssd @ w_br_ssd))
    return y @ w_out


def memory_cross_attention(hn, memn, w_q, w_kv, w_o):
    bsz, s, _ = hn.shape
    m = memn.shape[1]
    q = (hn @ w_q).reshape(bsz, s, XATTN_HEADS, XATTN_HEAD_DIM)
    kv = (memn @ w_kv).reshape(bsz, m, 2, XATTN_HEADS, XATTN_HEAD_DIM)
    k, v = kv[:, :, 0], kv[:, :, 1]
    scores = jnp.einsum("bshd,bmhd->bhsm", q, k, preferred_element_type=jnp.float32) * (XATTN_HEAD_DIM ** -0.5)
    probs = jax.nn.softmax(scores, axis=-1)
    o = jnp.einsum("bhsm,bmhd->bshd", probs.astype(v.dtype), v)
    return o.reshape(bsz, s, XATTN_HEADS * XATTN_HEAD_DIM) @ w_o


def setup_inputs(seed: int = 0) -> dict:
    key = jax.random.key(seed)
    keys = iter(jax.random.split(key, 48))
    f32 = jnp.float32

    def normal(shape, scale):
        return jax.random.normal(next(keys), shape, f32) * scale

    def gain(shape):
        return 1.0 + normal(shape, 0.02)

    L, D, F = DEPTH, D_MODEL, FFN_HIDDEN
    x = normal((BATCH, SEQ, D), 1.0)
    mem = normal((BATCH, MEM_LEN, D), 1.0)
    ffn1_norm = gain((L, D))
    ffn1_w_gu = normal((L, D, 2 * F), D ** -0.5)
    ffn1_w_down = normal((L, F, D), F ** -0.5)
    mix_norm = gain((L, D))
    w_in = normal((L, D, IN_WIDTH), D ** -0.5)
    lru_conv_w = normal((L, CONV_WIDTH, LRU_WIDTH), CONV_WIDTH ** -0.5)
    lru_conv_b = normal((L, LRU_WIDTH), 0.01)
    lru_w_a = normal((L, LRU_BLOCKS, LRU_BLOCK_SIZE, LRU_BLOCK_SIZE), LRU_BLOCK_SIZE ** -0.5)
    lru_b_a = normal((L, LRU_WIDTH), 0.01)
    lru_w_x = normal((L, LRU_BLOCKS, LRU_BLOCK_SIZE, LRU_BLOCK_SIZE), LRU_BLOCK_SIZE ** -0.5)
    lru_b_x = normal((L, LRU_WIDTH), 0.01)
    a_pow_c = jax.random.uniform(next(keys), (L, LRU_WIDTH), f32, 0.9, 0.999)
    a_base = a_pow_c ** (1.0 / LRU_C)
    lru_lambda = jnp.log(a_base) - jnp.log1p(-a_base)
    ssd_conv_w = normal((L, CONV_WIDTH, SSD_CONV_DIM), CONV_WIDTH ** -0.5)
    ssd_conv_b = normal((L, SSD_CONV_DIM), 0.01)
    dt0 = jnp.exp(jax.random.uniform(next(keys), (L, SSD_HEADS), f32, math.log(1e-3), math.log(1e-1)))
    ssd_dt_bias = dt0 + jnp.log(-jnp.expm1(-dt0))
    ssd_a_log = jnp.log(jax.random.uniform(next(keys), (L, SSD_HEADS), f32, 1.0, 16.0))
    ssd_d = gain((L, SSD_HEADS))
    ssd_norm = gain((L, SSD_D_INNER))
    gate_up = normal((L, GATE_RANK, N_BRANCHES * D), GATE_RANK ** -0.5)
    gate_b = normal((L, N_BRANCHES * D), 0.01)
    w_br_attn = normal((L, ATT_OUT_WIDTH, D), ATT_OUT_WIDTH ** -0.5)
    w_br_lru = normal((L, LRU_WIDTH, D), LRU_WIDTH ** -0.5)
    w_br_ssd = normal((L, SSD_D_INNER, D), SSD_D_INNER ** -0.5)
    w_out = normal((L, D, D), D ** -0.5)
    xattn_norm = gain((L, D))
    mem_norm = gain((L, D))
    xattn_w_q = normal((L, D, XATTN_HEADS * XATTN_HEAD_DIM), D ** -0.5)
    xattn_w_kv = normal((L, D, 2 * XATTN_HEADS * XATTN_HEAD_DIM), D ** -0.5)
    xattn_w_o = normal((L, XATTN_HEADS * XATTN_HEAD_DIM, D), (XATTN_HEADS * XATTN_HEAD_DIM) ** -0.5)
    ffn2_norm = gain((L, D))
    ffn2_w_gu = normal((L, D, 2 * F), D ** -0.5)
    ffn2_w_down = normal((L, F, D), F ** -0.5)
    final_norm = gain((D,))
    return {
        "x": x, "mem": mem,
        "ffn1_norm": ffn1_norm, "ffn1_w_gu": ffn1_w_gu, "ffn1_w_down": ffn1_w_down,
        "mix_norm": mix_norm, "w_in": w_in,
        "lru_conv_w": lru_conv_w, "lru_conv_b": lru_conv_b, "lru_w_a": lru_w_a, "lru_b_a": lru_b_a,
        "lru_w_x": lru_w_x, "lru_b_x": lru_b_x, "lru_lambda": lru_lambda,
        "ssd_conv_w": ssd_conv_w, "ssd_conv_b": ssd_conv_b, "ssd_dt_bias": ssd_dt_bias,
        "ssd_a_log": ssd_a_log, "ssd_d": ssd_d, "ssd_norm": ssd_norm,
        "gate_up": gate_up, "gate_b": gate_b,
        "w_br_attn": w_br_attn, "w_br_lru": w_br_lru, "w_br_ssd": w_br_ssd, "w_out": w_out,
        "xattn_norm": xattn_norm, "mem_norm": mem_norm, "xattn_w_q": xattn_w_q,
        "xattn_w_kv": xattn_w_kv, "xattn_w_o": xattn_w_o,
        "ffn2_norm": ffn2_norm, "ffn2_w_gu": ffn2_w_gu, "ffn2_w_down": ffn2_w_down,
        "final_norm": final_norm,
    }


def reference(x, mem, ffn1_norm, ffn1_w_gu, ffn1_w_down, mix_norm, w_in,
              lru_conv_w, lru_conv_b, lru_w_a, lru_b_a, lru_w_x, lru_b_x, lru_lambda,
              ssd_conv_w, ssd_conv_b, ssd_dt_bias, ssd_a_log, ssd_d, ssd_norm,
              gate_up, gate_b, w_br_attn, w_br_lru, w_br_ssd, w_out,
              xattn_norm, mem_norm, xattn_w_q, xattn_w_kv, xattn_w_o,
              ffn2_norm, ffn2_w_gu, ffn2_w_down, final_norm):
    h = x
    for l in range(DEPTH):
        h = h + 0.5 * swiglu(rms_norm(h, ffn1_norm[l]), ffn1_w_gu[l], ffn1_w_down[l])
        h = h + hybrid_mixer(rms_norm(h, mix_norm[l]), w_in[l],
                             lru_conv_w[l], lru_conv_b[l], lru_w_a[l], lru_b_a[l],
                             lru_w_x[l], lru_b_x[l], lru_lambda[l],
                             ssd_conv_w[l], ssd_conv_b[l], ssd_dt_bias[l], ssd_a_log[l],
                             ssd_d[l], ssd_norm[l],
                             gate_up[l], gate_b[l], w_br_attn[l], w_br_lru[l], w_br_ssd[l], w_out[l])
        h = h + memory_cross_attention(rms_norm(h, xattn_norm[l]), rms_norm(mem, mem_norm[l]),
                                       xattn_w_q[l], xattn_w_kv[l], xattn_w_o[l])
        h = h + 0.5 * swiglu(rms_norm(h, ffn2_norm[l]), ffn2_w_gu[l], ffn2_w_down[l])
    return rms_norm(h, final_norm)
```

```python
import functools

import jax
import jax.numpy as jnp
from jax import lax
from jax.experimental import pallas as pl
from jax.experimental.pallas import tpu as pltpu

F32 = jnp.float32
BF16 = jnp.bfloat16

HEAD_DIM = 128
ATT_PATTERNS = ((128, 1), (512, 4), (2048, 16))
ATT_GROUPS = 3
ATT_GROUP_HEADS = 4
ATT_HEADS = ATT_GROUPS * ATT_GROUP_HEADS
ATT_GROUP_WIDTH = ATT_GROUP_HEADS * HEAD_DIM
ATT_WIDTH = ATT_HEADS * HEAD_DIM
BAND = 128
LRU_BLOCK = 128
LRU_C = 8.0
CONV_WIDTH = 4
SSD_HEAD_DIM = 64
SSD_GROUPS = 4
SSD_STATE = 128
SSD_CHUNK = 128
GATE_RANK = 512
XATTN_HEADS = 4
XATTN_HEAD_DIM = 128
NORM_EPS = 1e-6

LANES = 128
SUBLANES = 8
VMEM_LIMIT_BYTES = 56 * 1024 * 1024


def _params(*semantics):
    return pltpu.CompilerParams(dimension_semantics=semantics, vmem_limit_bytes=VMEM_LIMIT_BYTES)


def _softplus(x):
    return jnp.maximum(x, 0.0) + jnp.log1p(jnp.exp(-jnp.abs(x)))


def _silu(x):
    return x * jax.nn.sigmoid(x)


def _dot(a, b):
    return jnp.dot(a, b, preferred_element_type=F32)


def _dot_nt(a, b):
    return lax.dot_general(a, b, (((1,), (1,)), ((), ())), preferred_element_type=F32)


def _rmsnorm_kernel(x_ref, g_ref, o_ref):
    x = x_ref[...]
    ms = jnp.mean(x * x, axis=-1, keepdims=True)
    o_ref[...] = (x * lax.rsqrt(ms + NORM_EPS) * g_ref[...]).astype(o_ref.dtype)


def _rmsnorm(x, g, out_dtype, rows=256):
    t, d = x.shape
    rows = min(rows, t)
    return pl.pallas_call(
        _rmsnorm_kernel,
        grid=(t // rows,),
        in_specs=[pl.BlockSpec((rows, d), lambda i: (i, 0)),
                  pl.BlockSpec((1, d), lambda i: (0, 0))],
        out_specs=pl.BlockSpec((rows, d), lambda i: (i, 0)),
        out_shape=jax.ShapeDtypeStruct((t, d), out_dtype),
        compiler_params=_params("parallel"),
    )(x, g.reshape(1, d))


def _mm_kernel(x_ref, w_ref, o_ref):
    o_ref[...] = _dot(x_ref[...], w_ref[...]).astype(o_ref.dtype)


def _mm(x, w, out_dtype, tm=1024, tn=1024):
    m, k = x.shape
    n = w.shape[1]
    tm, tn = min(tm, m), min(tn, n)
    return pl.pallas_call(
        _mm_kernel,
        grid=(m // tm, n // tn),
        in_specs=[pl.BlockSpec((tm, k), lambda i, j: (i, 0)),
                  pl.BlockSpec((k, tn), lambda i, j: (0, j))],
        out_specs=pl.BlockSpec((tm, tn), lambda i, j: (i, j)),
        out_shape=jax.ShapeDtypeStruct((m, n), out_dtype),
        compiler_params=_params("parallel", "parallel"),
    )(x, w)


def _mm_swiglu_kernel(x_ref, wg_ref, wu_ref, o_ref):
    x = x_ref[...]
    gate = _dot(x, wg_ref[...])
    up = _dot(x, wu_ref[...])
    o_ref[...] = (_silu(gate) * up).astype(o_ref.dtype)


def _mm_swiglu(x, w_gu, tm=1024, tn=512):
    m, k = x.shape
    f = w_gu.shape[1] // 2
    tm, tn = min(tm, m), min(tn, f)
    nf = f // tn
    return pl.pallas_call(
        _mm_swiglu_kernel,
        grid=(m // tm, nf),
        in_specs=[pl.BlockSpec((tm, k), lambda i, j: (i, 0)),
                  pl.BlockSpec((k, tn), lambda i, j: (0, j)),
                  pl.BlockSpec((k, tn), lambda i, j: (0, j + nf))],
        out_specs=pl.BlockSpec((tm, tn), lambda i, j: (i, j)),
        out_shape=jax.ShapeDtypeStruct((m, f), BF16),
        compiler_params=_params("parallel", "parallel"),
    )(x, w_gu, w_gu)


def _mm_residual_kernel(x_ref, w_ref, r_ref, o_ref, *, scale):
    o_ref[...] = r_ref[...] + scale * _dot(x_ref[...], w_ref[...])


def _mm_residual(x, w, res, scale, tm=1024, tn=512):
    m, k = x.shape
    n = w.shape[1]
    tm, tn = min(tm, m), min(tn, n)
    return pl.pallas_call(
        functools.partial(_mm_residual_kernel, scale=scale),
        grid=(m // tm, n // tn),
        in_specs=[pl.BlockSpec((tm, k), lambda i, j: (i, 0)),
                  pl.BlockSpec((k, tn), lambda i, j: (0, j)),
                  pl.BlockSpec((tm, tn), lambda i, j: (i, j))],
        out_specs=pl.BlockSpec((tm, tn), lambda i, j: (i, j)),
        out_shape=jax.ShapeDtypeStruct((m, n), F32),
        compiler_params=_params("parallel", "parallel"),
    )(x, w, res)


def _dilated_attn_kernel(q_ref, kp_ref, kc_ref, vp_ref, vc_ref, o_ref, lse_ref, *, dilation, slopes):
    n = pl.program_id(2)
    row = lax.broadcasted_iota(jnp.int32, (BAND, BAND), 0)
    col = lax.broadcasted_iota(jnp.int32, (BAND, BAND), 1)
    lag_c = row - col
    lag_p = lag_c + BAND
    valid_c = lag_c >= 0
    valid_p = lag_p <= jnp.where(n > 0, BAND, -1)
    lag_c_f = (lag_c * dilation).astype(F32)
    lag_p_f = (lag_p * dilation).astype(F32)
    scale = HEAD_DIM ** -0.5
    for hh in range(ATT_GROUP_HEADS):
        sl = slice(hh * HEAD_DIM, (hh + 1) * HEAD_DIM)
        q = q_ref[0, :, sl]
        s_c = _dot_nt(q, kc_ref[0, :, sl]) * scale - slopes[hh] * lag_c_f
        s_p = _dot_nt(q, kp_ref[0, :, sl]) * scale - slopes[hh] * lag_p_f
        s_c = jnp.where(valid_c, s_c, -jnp.inf)
        s_p = jnp.where(valid_p, s_p, -jnp.inf)
        m = jnp.maximum(jnp.max(s_c, axis=1, keepdims=True), jnp.max(s_p, axis=1, keepdims=True))
        p_c = jnp.exp(s_c - m)
        p_p = jnp.exp(s_p - m)
        denom = jnp.sum(p_c, axis=1, keepdims=True) + jnp.sum(p_p, axis=1, keepdims=True)
        out = _dot(p_c.astype(BF16), vc_ref[0, :, sl]) + _dot(p_p.astype(BF16), vp_ref[0, :, sl])
        o_ref[0, :, sl] = out / denom
        lse_ref[0, :, sl] = jnp.broadcast_to(m + jnp.log(denom), (BAND, HEAD_DIM))


def _dilated_attention(qkv, batch, seq, group):
    window, dilation = ATT_PATTERNS[group]
    assert window // dilation == BAND
    width = qkv.shape[1]
    wblocks = width // ATT_GROUP_WIDTH
    sub_len = seq // dilation
    nb = sub_len // BAND
    view = qkv.reshape(batch, sub_len, dilation * width)
    kblk = ATT_WIDTH // ATT_GROUP_WIDTH
    slopes = tuple(2.0 ** (-8.0 * (group * ATT_GROUP_HEADS + hh + 1) / ATT_HEADS)
                   for hh in range(ATT_GROUP_HEADS))

    def spec(col_block, prev):
        if prev:
            return pl.BlockSpec((1, BAND, ATT_GROUP_WIDTH),
                                lambda b, r, n: (b, jnp.maximum(n - 1, 0), r * wblocks + col_block))
        return pl.BlockSpec((1, BAND, ATT_GROUP_WIDTH), lambda b, r, n: (b, n, r * wblocks + col_block))

    out_spec = pl.BlockSpec((1, BAND, ATT_GROUP_WIDTH), lambda b, r, n: (b, n, r))
    out_shape = jax.ShapeDtypeStruct((batch, sub_len, dilation * ATT_GROUP_WIDTH), F32)
    o, lse = pl.pallas_call(
        functools.partial(_dilated_attn_kernel, dilation=dilation, slopes=slopes),
        grid=(batch, dilation, nb),
        in_specs=[spec(group, False),
                  spec(kblk + group, True), spec(kblk + group, False),
                  spec(2 * kblk + group, True), spec(2 * kblk + group, False)],
        out_specs=[out_spec, out_spec],
        out_shape=[out_shape, out_shape],
        compiler_params=_params("parallel", "parallel", "arbitrary"),
    )(view, view, view, view, view)
    t = batch * seq
    return o.reshape(t, ATT_GROUP_WIDTH), lse.reshape(t, ATT_GROUP_WIDTH)


def _attn_mix_kernel(o0_ref, o1_ref, o2_ref, l0_ref, l1_ref, l2_ref, y_ref):
    l0, l1, l2 = l0_ref[...], l1_ref[...], l2_ref[...]
    m = jnp.maximum(jnp.maximum(l0, l1), l2)
    e0, e1, e2 = jnp.exp(l0 - m), jnp.exp(l1 - m), jnp.exp(l2 - m)
    total = e0 + e1 + e2
    y = (e0 * o0_ref[...] + e1 * o1_ref[...] + e2 * o2_ref[...]) / total
    y_ref[...] = y.astype(y_ref.dtype)


def _attn_mix(outs, lses, rows=1024):
    t, w = outs[0].shape
    rows = min(rows, t)
    spec = pl.BlockSpec((rows, w), lambda i: (i, 0))
    return pl.pallas_call(
        _attn_mix_kernel,
        grid=(t // rows,),
        in_specs=[spec] * 6,
        out_specs=spec,
        out_shape=jax.ShapeDtypeStruct((t, w), BF16),
        compiler_params=_params("parallel"),
    )(*outs, *lses)


def _causal_conv(x, tail_ref, w_ref, b_ref):
    rows = x.shape[0]
    ext = jnp.concatenate([tail_ref[...], x], axis=0)
    y = w_ref[CONV_WIDTH - 1:CONV_WIDTH, :] * x + b_ref[...]
    for back in range(1, CONV_WIDTH):
        shifted = pltpu.roll(ext, back, axis=0)[SUBLANES:SUBLANES + rows]
        y = y + w_ref[CONV_WIDTH - 1 - back:CONV_WIDTH - back, :] * shifted
    tail_ref[...] = x[rows - SUBLANES:rows]
    return y


def _lru_kernel(x_ref, g_ref, cw_ref, cb_ref, wa_ref, ba_ref, wx_ref, bx_ref, lam_ref, o_ref,
                tail_ref, h_ref):
    @pl.when(pl.program_id(2) == 0)
    def _():
        tail_ref[...] = jnp.zeros_like(tail_ref)
        h_ref[...] = jnp.zeros_like(h_ref)

    rows, width = x_ref.shape[1], x_ref.shape[2]
    xc = _causal_conv(x_ref[0], tail_ref, cw_ref, cb_ref)
    xb = xc.astype(BF16)
    pre_a, pre_x = [], []
    for blk in range(width // LRU_BLOCK):
        sl = slice(blk * LRU_BLOCK, (blk + 1) * LRU_BLOCK)
        pre_a.append(_dot(xb[:, sl], wa_ref[blk]))
        pre_x.append(_dot(xb[:, sl], wx_ref[blk]))
    r = jax.nn.sigmoid(jnp.concatenate(pre_a, axis=1) + ba_ref[...])
    i = jax.nn.sigmoid(jnp.concatenate(pre_x, axis=1) + bx_ref[...])
    log_a = -LRU_C * r * _softplus(-lam_ref[...])
    a = jnp.exp(log_a)
    b = jnp.sqrt(1.0 - jnp.exp(2.0 * log_a)) * (i * xc)
    row = lax.broadcasted_iota(jnp.int32, (rows, width), 0)
    step = 1
    while step < rows:
        keep = row >= step
        a_prev = pltpu.roll(a, step, axis=0)
        b_prev = pltpu.roll(b, step, axis=0)
        b = jnp.where(keep, a * b_prev + b, b)
        a = jnp.where(keep, a * a_prev, a)
        step *= 2
    h = a * h_ref[...] + b
    h_ref[...] = h[rows - 1:rows]
    o_ref[0] = (jax.nn.gelu(g_ref[0]) * h).astype(o_ref.dtype)


def _lru_branch(proj, batch, seq, x_col, g_col, width, p, rows=256, cols=512):
    rows, cols = min(rows, seq), min(cols, width)
    view = proj.reshape(batch, seq, proj.shape[1])
    nblk = cols // LRU_BLOCK
    vec = lambda: pl.BlockSpec((1, cols), lambda b, c, t: (0, c))
    out = pl.pallas_call(
        _lru_kernel,
        grid=(batch, width // cols, seq // rows),
        in_specs=[pl.BlockSpec((1, rows, cols), lambda b, c, t: (b, t, x_col // cols + c)),
                  pl.BlockSpec((1, rows, cols), lambda b, c, t: (b, t, g_col // cols + c)),
                  pl.BlockSpec((CONV_WIDTH, cols), lambda b, c, t: (0, c)),
                  vec(),
                  pl.BlockSpec((nblk, LRU_BLOCK, LRU_BLOCK), lambda b, c, t: (c, 0, 0)),
                  vec(),
                  pl.BlockSpec((nblk, LRU_BLOCK, LRU_BLOCK), lambda b, c, t: (c, 0, 0)),
                  vec(), vec()],
        out_specs=pl.BlockSpec((1, rows, cols), lambda b, c, t: (b, t, c)),
        out_shape=jax.ShapeDtypeStruct((batch, seq, width), BF16),
        scratch_shapes=[pltpu.VMEM((SUBLANES, cols), F32), pltpu.VMEM((1, cols), F32)],
        compiler_params=_params("parallel", "parallel", "arbitrary"),
    )(view, view, p["conv_w"], p["conv_b"].reshape(1, width), p["w_a"], p["b_a"].reshape(1, width),
      p["w_x"], p["b_x"].reshape(1, width), p["lam"].reshape(1, width))
    return out.reshape(batch * seq, width)


def _ssd_kernel(x_ref, bc_ref, z_ref, dt_ref, xw_ref, xb_ref, bcw_ref, bcb_ref, dtb_ref, alog_ref, dskip_ref,
                ng_ref, o_ref, xtail_ref, bctail_ref, st_ref):
    @pl.when(pl.program_id(1) == 0)
    def _():
        xtail_ref[...] = jnp.zeros_like(xtail_ref)
        bctail_ref[...] = jnp.zeros_like(bctail_ref)
        st_ref[...] = jnp.zeros_like(st_ref)

    chunk = SSD_CHUNK
    gw = SSD_GROUPS * SSD_STATE
    d_inner = x_ref.shape[2]
    group_width = d_inner // SSD_GROUPS
    pairs = group_width // LANES
    xs = _silu(_causal_conv(x_ref[0], xtail_ref, xw_ref, xb_ref))
    bc = _silu(_causal_conv(bc_ref[0], bctail_ref, bcw_ref, bcb_ref))
    bm = bc[:, :gw]
    cm = bc[:, gw:]

    dt = _softplus(dt_ref[0] + dtb_ref[...])
    da = dt * (-jnp.exp(alog_ref[...]))
    row = lax.broadcasted_iota(jnp.int32, (chunk, LANES), 0)
    col = lax.broadcasted_iota(jnp.int32, (chunk, LANES), 1)
    cs = da
    step = 1
    while step < chunk:
        cs = cs + jnp.where(row >= step, pltpu.roll(cs, step, axis=0), 0.0)
        step *= 2
    cs_t = cs.T
    dt_t = dt.T
    to_end_t = jnp.exp(cs_t[:, chunk - 1:chunk] - cs_t) * dt_t
    from_start = jnp.exp(cs)
    causal = row >= col
    left = col < SSD_HEAD_DIM

    ys = []
    for g in range(SSD_GROUPS):
        gs = slice(g * SSD_STATE, (g + 1) * SSD_STATE)
        b_g, c_g = bm[:, gs], cm[:, gs]
        cb = _dot_nt(c_g.astype(BF16), b_g.astype(BF16))
        b_t = b_g.T
        for pr in range(pairs):
            ps = slice(g * group_width + pr * LANES, g * group_width + (pr + 1) * LANES)
            x_pair = xs[:, ps]
            st_pair = st_ref[g, :, pr * LANES:(pr + 1) * LANES]
            rhs = jnp.concatenate([x_pair, st_pair], axis=0).astype(BF16)
            x_pair_b = x_pair.astype(BF16)
            y_half, st_half = [], []
            for half in range(2):
                h = (g * pairs + pr) * 2 + half
                decay = jnp.where(causal, jnp.exp(cs[:, h:h + 1] - cs_t[h:h + 1, :]), 0.0)
                m_h = decay * cb * dt_t[h:h + 1, :]
                c_h = c_g * from_start[:, h:h + 1]
                lhs = jnp.concatenate([m_h, c_h], axis=1).astype(BF16)
                y_half.append(_dot(lhs, rhs))
                local = _dot((b_t * to_end_t[h:h + 1, :]).astype(BF16), x_pair_b)
                st_half.append(from_start[chunk - 1:chunk, h:h + 1] * st_pair + local)
            ys.append(jnp.where(left, y_half[0], y_half[1]))
            st_ref[g, :, pr * LANES:(pr + 1) * LANES] = jnp.where(left, st_half[0], st_half[1])
    y = jnp.concatenate(ys, axis=1)
    y = y + xs * dskip_ref[...]
    y = y * _silu(z_ref[0])
    outs = []
    for g in range(SSD_GROUPS):
        yg = y[:, g * group_width:(g + 1) * group_width]
        ms = jnp.mean(yg * yg, axis=-1, keepdims=True)
        outs.append(yg * lax.rsqrt(ms + NORM_EPS))
    o_ref[0] = (jnp.concatenate(outs, axis=1) * ng_ref[...]).astype(o_ref.dtype)


def _ssd_branch(proj, dt_proj, batch, seq, z_col, xbc_col, d_inner, p):
    bc_width = 2 * SSD_GROUPS * SSD_STATE
    bc_col = xbc_col + d_inner
    heads = d_inner // SSD_HEAD_DIM
    view = proj.reshape(batch, seq, proj.shape[1])
    dt_view = dt_proj.reshape(batch, seq, LANES)
    pad = lambda v: jnp.pad(v, (0, LANES - heads)).reshape(1, LANES)
    const = lambda shape: pl.BlockSpec(shape, lambda b, c: (0, 0))
    conv_w, conv_b = p["conv_w"], p["conv_b"].reshape(1, d_inner + bc_width)
    out = pl.pallas_call(
        _ssd_kernel,
        grid=(batch, seq // SSD_CHUNK),
        in_specs=[pl.BlockSpec((1, SSD_CHUNK, d_inner), lambda b, c: (b, c, xbc_col // d_inner)),
                  pl.BlockSpec((1, SSD_CHUNK, bc_width), lambda b, c: (b, c, bc_col // bc_width)),
                  pl.BlockSpec((1, SSD_CHUNK, d_inner), lambda b, c: (b, c, z_col // d_inner)),
                  pl.BlockSpec((1, SSD_CHUNK, LANES), lambda b, c: (b, c, 0)),
                  const((CONV_WIDTH, d_inner)), const((1, d_inner)),
                  const((CONV_WIDTH, bc_width)), const((1, bc_width)),
                  const((1, LANES)), const((1, LANES)), const((1, d_inner)), const((1, d_inner))],
        out_specs=pl.BlockSpec((1, SSD_CHUNK, d_inner), lambda b, c: (b, c, 0)),
        out_shape=jax.ShapeDtypeStruct((batch, seq, d_inner), BF16),
        scratch_shapes=[pltpu.VMEM((SUBLANES, d_inner), F32), pltpu.VMEM((SUBLANES, bc_width), F32),
                        pltpu.VMEM((SSD_GROUPS, SSD_STATE, d_inner // SSD_GROUPS), F32)],
        compiler_params=_params("parallel", "arbitrary"),
    )(view, view, view, dt_view, conv_w[:, :d_inner], conv_b[:, :d_inner], conv_w[:, d_inner:],
      conv_b[:, d_inner:], pad(p["dt_bias"]), pad(p["a_log"]),
      jnp.repeat(p["d_skip"], SSD_HEAD_DIM).reshape(1, d_inner), p["norm"].reshape(1, d_inner))
    return out.reshape(batch * seq, d_inner)


def _merge_kernel(gl_ref, ya_ref, yl_ref, ys_ref, gu0_ref, gu1_ref, gu2_ref, gb0_ref, gb1_ref, gb2_ref,
                  pa_ref, pl_ref, ps_ref, o_ref):
    g_low = gl_ref[...]

    def gated(gu_ref, gb_ref, y_ref, p_ref):
        return jax.nn.sigmoid(_dot(g_low, gu_ref[...]) + gb_ref[...]) * _dot(y_ref[...], p_ref[...])

    y = (gated(gu0_ref, gb0_ref, ya_ref, pa_ref) + gated(gu1_ref, gb1_ref, yl_ref, pl_ref)
         + gated(gu2_ref, gb2_ref, ys_ref, ps_ref))
    o_ref[...] = y.astype(o_ref.dtype)


def _gated_merge(qkvg, g_col, y_attn, y_lru, y_ssd, gate_up, gate_b, p_attn, p_lru, p_ssd, tm=1024, tn=512):
    m = y_attn.shape[0]
    d = p_attn.shape[1]
    tm, tn = min(tm, m), min(tn, d)
    nd = d // tn
    rows = lambda width, col_block=0: pl.BlockSpec((tm, width), lambda i, j: (i, col_block))
    gate_w = lambda br: pl.BlockSpec((GATE_RANK, tn), lambda i, j: (0, br * nd + j))
    gate_bias = lambda br: pl.BlockSpec((1, tn), lambda i, j: (0, br * nd + j))
    branch_w = lambda k: pl.BlockSpec((k, tn), lambda i, j: (0, j))
    gb = gate_b.reshape(1, 3 * d)
    return pl.pallas_call(
        _merge_kernel,
        grid=(m // tm, nd),
        in_specs=[rows(GATE_RANK, g_col // GATE_RANK), rows(y_attn.shape[1]), rows(y_lru.shape[1]),
                  rows(y_ssd.shape[1]),
                  gate_w(0), gate_w(1), gate_w(2), gate_bias(0), gate_bias(1), gate_bias(2),
                  branch_w(p_attn.shape[0]), branch_w(p_lru.shape[0]), branch_w(p_ssd.shape[0])],
        out_specs=pl.BlockSpec((tm, tn), lambda i, j: (i, j)),
        out_shape=jax.ShapeDtypeStruct((m, d), BF16),
        compiler_params=_params("parallel", "parallel"),
    )(qkvg, y_attn, y_lru, y_ssd, gate_up, gate_up, gate_up, gb, gb, gb, p_attn, p_lru, p_ssd)


def _xattn_kernel(h_ref, g_ref, wq_ref, kv_ref, wo_ref, o_ref):
    h = h_ref[0]
    ms = jnp.mean(h * h, axis=-1, keepdims=True)
    hn = (h * lax.rsqrt(ms + NORM_EPS) * g_ref[...]).astype(BF16)
    q = _dot(hn, wq_ref[...]).astype(BF16)
    width = XATTN_HEADS * XATTN_HEAD_DIM
    scale = XATTN_HEAD_DIM ** -0.5
    outs = []
    for hh in range(XATTN_HEADS):
        sl = slice(hh * XATTN_HEAD_DIM, (hh + 1) * XATTN_HEAD_DIM)
        vsl = slice(width + hh * XATTN_HEAD_DIM, width + (hh + 1) * XATTN_HEAD_DIM)
        s = _dot_nt(q[:, sl], kv_ref[0, :, sl]) * scale
        p = jnp.exp(s - jnp.max(s, axis=1, keepdims=True))
        denom = jnp.sum(p, axis=1, keepdims=True)
        outs.append(_dot(p.astype(BF16), kv_ref[0, :, vsl]) / denom)
    o = jnp.concatenate(outs, axis=1).astype(BF16)
    o_ref[0] = h + _dot(o, wo_ref[...])


def _memory_cross_attention(h, batch, seq, norm_g, w_q, kv, w_o, rows=256):
    d = h.shape[1]
    rows = min(rows, seq)
    width = XATTN_HEADS * XATTN_HEAD_DIM
    mem_len = kv.shape[0] // batch
    out = pl.pallas_call(
        _xattn_kernel,
        grid=(batch, seq // rows),
        in_specs=[pl.BlockSpec((1, rows, d), lambda b, i: (b, i, 0)),
                  pl.BlockSpec((1, d), lambda b, i: (0, 0)),
                  pl.BlockSpec((d, width), lambda b, i: (0, 0)),
                  pl.BlockSpec((1, mem_len, 2 * width), lambda b, i: (b, 0, 0)),
                  pl.BlockSpec((width, d), lambda b, i: (0, 0))],
        out_specs=pl.BlockSpec((1, rows, d), lambda b, i: (b, i, 0)),
        out_shape=jax.ShapeDtypeStruct((batch, seq, d), F32),
        compiler_params=_params("parallel", "parallel"),
    )(h.reshape(batch, seq, d), norm_g.reshape(1, d), w_q, kv.reshape(batch, mem_len, 2 * width), w_o)
    return out.reshape(batch * seq, d)


def _ffn(h, norm_g, w_gu, w_down):
    hn = _rmsnorm(h, norm_g, BF16)
    hidden = _mm_swiglu(hn, w_gu.astype(BF16))
    return _mm_residual(hidden, w_down.astype(BF16), h, 0.5)


def _layer(h, mem, p, batch, seq):
    d = h.shape[1]
    lru_width = p["lru_conv_w"].shape[1]
    d_inner = p["ssd_norm"].shape[0]
    conv_dim = p["ssd_conv_w"].shape[1]
    heads = p["ssd_dt_bias"].shape[0]

    h = _ffn(h, p["ffn1_norm"], p["ffn1_w_gu"], p["ffn1_w_down"])

    u = _rmsnorm(h, p["mix_norm"], BF16)
    w_in = p["w_in"]
    c_lru = 3 * ATT_WIDTH
    c_dt = c_lru + 2 * lru_width + d_inner + conv_dim
    c_gate = c_dt + heads
    w_a = jnp.concatenate([w_in[:, :c_lru], w_in[:, c_gate:]], axis=1).astype(BF16)
    w_b = w_in[:, c_lru:c_dt].astype(BF16)
    w_dt = jnp.pad(w_in[:, c_dt:c_gate], ((0, 0), (0, LANES - heads))).astype(BF16)
    qkvg = _mm(u, w_a, BF16)
    proj = _mm(u, w_b, F32)
    dt_proj = _mm(u, w_dt, F32)

    attn = [_dilated_attention(qkvg, batch, seq, g) for g in range(ATT_GROUPS)]
    y_attn = _attn_mix([o for o, _ in attn], [l for _, l in attn])
    y_lru = _lru_branch(proj, batch, seq, 0, lru_width, lru_width,
                        dict(conv_w=p["lru_conv_w"], conv_b=p["lru_conv_b"], w_a=p["lru_w_a"].astype(BF16),
                             b_a=p["lru_b_a"], w_x=p["lru_w_x"].astype(BF16), b_x=p["lru_b_x"],
                             lam=p["lru_lambda"]))
    y_ssd = _ssd_branch(proj, dt_proj, batch, seq, 2 * lru_width, 2 * lru_width + d_inner, d_inner,
                        dict(conv_w=p["ssd_conv_w"], conv_b=p["ssd_conv_b"], dt_bias=p["ssd_dt_bias"],
                             a_log=p["ssd_a_log"], d_skip=p["ssd_d"], norm=p["ssd_norm"]))
    merged = _gated_merge(qkvg, c_lru, y_attn, y_lru, y_ssd, p["gate_up"].astype(BF16), p["gate_b"],
                          p["w_br_attn"].astype(BF16), p["w_br_lru"].astype(BF16), p["w_br_ssd"].astype(BF16))
    h = _mm_residual(merged, p["w_out"].astype(BF16), h, 1.0)

    memn = _rmsnorm(mem, p["mem_norm"], BF16)
    kv = _mm(memn, p["xattn_w_kv"].astype(BF16), BF16)
    h = _memory_cross_attention(h, batch, seq, p["xattn_norm"], p["xattn_w_q"].astype(BF16), kv,
                                p["xattn_w_o"].astype(BF16))

    return _ffn(h, p["ffn2_norm"], p["ffn2_w_gu"], p["ffn2_w_down"])


def kernel(x, mem, ffn1_norm, ffn1_w_gu, ffn1_w_down, mix_norm, w_in, lru_conv_w, lru_conv_b, lru_w_a, lru_b_a, lru_w_x, lru_b_x, lru_lambda, ssd_conv_w, ssd_conv_b, ssd_dt_bias, ssd_a_log, ssd_d, ssd_norm, gate_up, gate_b, w_br_attn, w_br_lru, w_br_ssd, w_out, xattn_norm, mem_norm, xattn_w_q, xattn_w_kv, xattn_w_o, ffn2_norm, ffn2_w_gu, ffn2_w_down, final_norm):
    batch, seq, d = x.shape
    stacked = dict(
        ffn1_norm=ffn1_norm, ffn1_w_gu=ffn1_w_gu, ffn1_w_down=ffn1_w_down, mix_norm=mix_norm, w_in=w_in,
        lru_conv_w=lru_conv_w, lru_conv_b=lru_conv_b, lru_w_a=lru_w_a, lru_b_a=lru_b_a, lru_w_x=lru_w_x,
        lru_b_x=lru_b_x, lru_lambda=lru_lambda, ssd_conv_w=ssd_conv_w, ssd_conv_b=ssd_conv_b,
        ssd_dt_bias=ssd_dt_bias, ssd_a_log=ssd_a_log, ssd_d=ssd_d, ssd_norm=ssd_norm, gate_up=gate_up,
        gate_b=gate_b, w_br_attn=w_br_attn, w_br_lru=w_br_lru, w_br_ssd=w_br_ssd, w_out=w_out,
        xattn_norm=xattn_norm, mem_norm=mem_norm, xattn_w_q=xattn_w_q, xattn_w_kv=xattn_w_kv,
        xattn_w_o=xattn_w_o, ffn2_norm=ffn2_norm, ffn2_w_gu=ffn2_w_gu, ffn2_w_down=ffn2_w_down)
    mem2 = mem.reshape(batch * mem.shape[1], d)

    def body(h, p):
        return _layer(h, mem2, p, batch, seq), None

    h, _ = lax.scan(body, x.reshape(batch * seq, d), stacked)
    return _rmsnorm(h, final_norm, F32).reshape(batch, seq, d)
```

```python
import functools

import jax
import jax.numpy as jnp
from jax import lax
from jax.experimental import pallas as pl
from jax.experimental.pallas import tpu as pltpu

F32 = jnp.float32
BF16 = jnp.bfloat16

HEAD_DIM = 128
ATT_PATTERNS = ((128, 1), (512, 4), (2048, 16))
ATT_GROUPS = 3
ATT_GROUP_HEADS = 4
ATT_HEADS = ATT_GROUPS * ATT_GROUP_HEADS
ATT_GROUP_WIDTH = ATT_GROUP_HEADS * HEAD_DIM
ATT_WIDTH = ATT_HEADS * HEAD_DIM
BAND = 128
SEGMENT = 2048
LRU_BLOCK = 128
LRU_C = 8.0
CONV_WIDTH = 4
SSD_HEAD_DIM = 64
SSD_GROUPS = 4
SSD_STATE = 128
SSD_CHUNK = 128
GATE_RANK = 512
XATTN_HEADS = 4
XATTN_HEAD_DIM = 128
NORM_EPS = 1e-6

LANES = 128
SUBLANES = 8
VMEM_LIMIT_BYTES = 56 * 1024 * 1024


def _params(*semantics):
    return pltpu.CompilerParams(dimension_semantics=semantics, vmem_limit_bytes=VMEM_LIMIT_BYTES)


def _softplus(x):
    return jnp.maximum(x, 0.0) + jnp.log1p(jnp.exp(-jnp.abs(x)))


def _silu(x):
    return x * jax.nn.sigmoid(x)


def _dot(a, b):
    return jnp.dot(a, b, preferred_element_type=F32)


def _dot_nt(a, b):
    return lax.dot_general(a, b, (((1,), (1,)), ((), ())), preferred_element_type=F32)


def _rmsnorm_kernel(x_ref, g_ref, o_ref):
    x = x_ref[...]
    ms = jnp.mean(x * x, axis=-1, keepdims=True)
    o_ref[...] = (x * lax.rsqrt(ms + NORM_EPS) * g_ref[...]).astype(o_ref.dtype)


def _rmsnorm(x, g, out_dtype, rows=256):
    t, d = x.shape
    rows = min(rows, t)
    return pl.pallas_call(
        _rmsnorm_kernel,
        grid=(t // rows,),
        in_specs=[pl.BlockSpec((rows, d), lambda i: (i, 0)),
                  pl.BlockSpec((1, d), lambda i: (0, 0))],
        out_specs=pl.BlockSpec((rows, d), lambda i: (i, 0)),
        out_shape=jax.ShapeDtypeStruct((t, d), out_dtype),
        compiler_params=_params("parallel"),
    )(x, g.reshape(1, d))


def _row_rsqrt(ssq_ref, inv_d):
    return lax.rsqrt(ssq_ref[:, :1] * inv_d + NORM_EPS)


def _mm_kernel(*refs, inv_d):
    if inv_d is None:
        x_ref, w_ref, o_ref = refs
        o_ref[...] = _dot(x_ref[...], w_ref[...]).astype(o_ref.dtype)
    else:
        x_ref, w_ref, ssq_ref, o_ref = refs
        o_ref[...] = (_dot(x_ref[...], w_ref[...]) * _row_rsqrt(ssq_ref, inv_d)).astype(o_ref.dtype)


def _mm(x, w, out_dtype, ssq=None, tm=1024, tn=1024):
    m, k = x.shape
    n = w.shape[1]
    tm, tn = min(tm, m), min(tn, n)
    in_specs = [pl.BlockSpec((tm, k), lambda i, j: (i, 0)), pl.BlockSpec((k, tn), lambda i, j: (0, j))]
    args = [x, w]
    if ssq is not None:
        in_specs.append(pl.BlockSpec((tm, LANES), lambda i, j: (i, 0)))
        args.append(ssq)
    return pl.pallas_call(
        functools.partial(_mm_kernel, inv_d=None if ssq is None else 1.0 / k),
        grid=(m // tm, n // tn),
        in_specs=in_specs,
        out_specs=pl.BlockSpec((tm, tn), lambda i, j: (i, j)),
        out_shape=jax.ShapeDtypeStruct((m, n), out_dtype),
        compiler_params=_params("parallel", "parallel"),
    )(*args)


def _mm_swiglu_kernel(*refs, inv_d):
    if inv_d is None:
        x_ref, wg_ref, wu_ref, o_ref = refs
        scale = None
    else:
        x_ref, wg_ref, wu_ref, ssq_ref, o_ref = refs
        scale = _row_rsqrt(ssq_ref, inv_d)
    x = x_ref[...]
    gate = _dot(x, wg_ref[...])
    up = _dot(x, wu_ref[...])
    if scale is not None:
        gate, up = gate * scale, up * scale
    o_ref[...] = (_silu(gate) * up).astype(o_ref.dtype)


def _mm_swiglu(x, w_gu, ssq=None, tm=1024, tn=512):
    m, k = x.shape
    f = w_gu.shape[1] // 2
    tm, tn = min(tm, m), min(tn, f)
    nf = f // tn
    in_specs = [pl.BlockSpec((tm, k), lambda i, j: (i, 0)),
                pl.BlockSpec((k, tn), lambda i, j: (0, j)),
                pl.BlockSpec((k, tn), lambda i, j: (0, j + nf))]
    args = [x, w_gu, w_gu]
    if ssq is not None:
        in_specs.append(pl.BlockSpec((tm, LANES), lambda i, j: (i, 0)))
        args.append(ssq)
    return pl.pallas_call(
        functools.partial(_mm_swiglu_kernel, inv_d=None if ssq is None else 1.0 / k),
        grid=(m // tm, nf),
        in_specs=in_specs,
        out_specs=pl.BlockSpec((tm, tn), lambda i, j: (i, j)),
        out_shape=jax.ShapeDtypeStruct((m, f), BF16),
        compiler_params=_params("parallel", "parallel"),
    )(*args)


def _mm_residual_kernel(*refs, scale, emit):
    if emit:
        x_ref, w_ref, r_ref, g_ref, o_ref, hg_ref, ssq_ref = refs
    else:
        x_ref, w_ref, r_ref, o_ref = refs
    h = r_ref[...] + scale * _dot(x_ref[...], w_ref[...])
    o_ref[...] = h
    if emit:
        hg_ref[...] = (h * g_ref[...]).astype(hg_ref.dtype)
        part = jnp.broadcast_to(jnp.sum(h * h, axis=1, keepdims=True), ssq_ref.shape)

        @pl.when(pl.program_id(1) == 0)
        def _():
            ssq_ref[...] = part

        @pl.when(pl.program_id(1) > 0)
        def _():
            ssq_ref[...] += part


def _mm_residual(x, w, res, scale, next_gain=None, tm=1024, tn=512):
    m, k = x.shape
    n = w.shape[1]
    tm, tn = min(tm, m), min(tn, n)
    emit = next_gain is not None
    tile = pl.BlockSpec((tm, tn), lambda i, j: (i, j))
    in_specs = [pl.BlockSpec((tm, k), lambda i, j: (i, 0)), pl.BlockSpec((k, tn), lambda i, j: (0, j)), tile]
    args = [x, w, res]
    out_specs, out_shape = tile, jax.ShapeDtypeStruct((m, n), F32)
    if emit:
        in_specs.append(pl.BlockSpec((1, tn), lambda i, j: (0, j)))
        args.append(next_gain.reshape(1, n))
        out_specs = [tile, tile, pl.BlockSpec((tm, LANES), lambda i, j: (i, 0))]
        out_shape = [out_shape, jax.ShapeDtypeStruct((m, n), BF16), jax.ShapeDtypeStruct((m, LANES), F32)]
    return pl.pallas_call(
        functools.partial(_mm_residual_kernel, scale=scale, emit=emit),
        grid=(m // tm, n // tn),
        in_specs=in_specs,
        out_specs=out_specs,
        out_shape=out_shape,
        compiler_params=_params("parallel", "arbitrary"),
    )(*args)


def _gain_ssq_kernel(x_ref, g_ref, hg_ref, ssq_ref):
    x = x_ref[...]
    hg_ref[...] = (x * g_ref[...]).astype(hg_ref.dtype)
    ssq_ref[...] = jnp.broadcast_to(jnp.sum(x * x, axis=1, keepdims=True), ssq_ref.shape)


def _gain_ssq(x, g, rows=256):
    t, d = x.shape
    rows = min(rows, t)
    return pl.pallas_call(
        _gain_ssq_kernel,
        grid=(t // rows,),
        in_specs=[pl.BlockSpec((rows, d), lambda i: (i, 0)), pl.BlockSpec((1, d), lambda i: (0, 0))],
        out_specs=[pl.BlockSpec((rows, d), lambda i: (i, 0)), pl.BlockSpec((rows, LANES), lambda i: (i, 0))],
        out_shape=[jax.ShapeDtypeStruct((t, d), BF16), jax.ShapeDtypeStruct((t, LANES), F32)],
        compiler_params=_params("parallel"),
    )(x, g.reshape(1, d))


def _rows(start, dilation):
    return pl.ds(start, BAND, stride=dilation) if dilation > 1 else pl.ds(start, BAND)


def _attn_kernel(slope_ref, *refs):
    ins = refs[:5 * ATT_GROUPS]
    y_ref, o_scr, l_scr = refs[5 * ATT_GROUPS:]
    first_segment = pl.program_id(1) == 0
    row = lax.broadcasted_iota(jnp.int32, (BAND, BAND), 0)
    col = lax.broadcasted_iota(jnp.int32, (BAND, BAND), 1)
    lag_c = row - col
    lag_p = lag_c + BAND
    valid_c = lag_c >= 0
    valid_p_any = lag_p <= BAND
    valid_p_first = lag_p <= jnp.where(first_segment, -1, BAND)
    scale = HEAD_DIM ** -0.5
    for g, (window, dilation) in enumerate(ATT_PATTERNS):
        q_ref, kp_ref, kc_ref, vp_ref, vc_ref = ins[5 * g:5 * g + 5]
        slope = slope_ref[0, :, g:g + 1] * float(dilation)
        bias_c = slope * lag_c.astype(F32)
        bias_p = slope * lag_p.astype(F32)
        span = BAND * dilation
        for r in range(dilation):
            k_prev = kp_ref[0, _rows(r, dilation), :].astype(BF16)
            v_prev = vp_ref[0, _rows(r, dilation), :].astype(BF16)
            for blk in range(SEGMENT // span):
                rows = _rows(blk * span + r, dilation)
                q = q_ref[0, rows, :].astype(BF16)
                k_own = kc_ref[0, rows, :].astype(BF16)
                v_own = vc_ref[0, rows, :].astype(BF16)
                s_c = jnp.where(valid_c, _dot_nt(q, k_own) * scale - bias_c, -jnp.inf)
                s_p = jnp.where(valid_p_first if blk == 0 else valid_p_any,
                                _dot_nt(q, k_prev) * scale - bias_p, -jnp.inf)
                m = jnp.maximum(jnp.max(s_c, axis=1, keepdims=True), jnp.max(s_p, axis=1, keepdims=True))
                p_c = jnp.exp(s_c - m)
                p_p = jnp.exp(s_p - m)
                denom = jnp.sum(p_c, axis=1, keepdims=True) + jnp.sum(p_p, axis=1, keepdims=True)
                out = _dot(p_c.astype(BF16), v_own) + _dot(p_p.astype(BF16), v_prev)
                o_scr[g, rows, :] = out / denom
                l_scr[g, rows, :] = jnp.broadcast_to(m + jnp.log(denom), (BAND, HEAD_DIM))
                k_prev, v_prev = k_own, v_own
    l0, l1, l2 = l_scr[0], l_scr[1], l_scr[2]
    m = jnp.maximum(jnp.maximum(l0, l1), l2)
    e0, e1, e2 = jnp.exp(l0 - m), jnp.exp(l1 - m), jnp.exp(l2 - m)
    y = (e0 * o_scr[0] + e1 * o_scr[1] + e2 * o_scr[2]) / (e0 + e1 + e2)
    y_ref[0] = y.astype(y_ref.dtype)


def _dilated_attention(qkv, batch, seq):
    view = qkv.reshape(batch, seq, 3 * ATT_WIDTH)
    heads = ATT_WIDTH // HEAD_DIM
    slopes = jnp.asarray(
        [[[2.0 ** (-8.0 * (g * ATT_GROUP_HEADS + hh + 1) / ATT_HEADS) if g < ATT_GROUPS else 0.0
           for g in range(LANES)]] for hh in range(ATT_GROUP_HEADS)], F32)
    in_specs = [pl.BlockSpec((1, 1, LANES), lambda b, s, hh: (hh, 0, 0))]
    for g, (window, dilation) in enumerate(ATT_PATTERNS):
        assert window // dilation == BAND and SEGMENT % (BAND * dilation) == 0
        span = BAND * dilation
        per_seg = SEGMENT // span

        def cur(which, g=g):
            return pl.BlockSpec((1, SEGMENT, HEAD_DIM),
                                lambda b, s, hh: (b, s, which * heads + g * ATT_GROUP_HEADS + hh))

        def prev(which, g=g, span=span, per_seg=per_seg):
            return pl.BlockSpec((1, span, HEAD_DIM),
                                lambda b, s, hh: (b, jnp.maximum(s * per_seg - 1, 0),
                                                  which * heads + g * ATT_GROUP_HEADS + hh))

        in_specs += [cur(0), prev(1), cur(1), prev(2), cur(2)]
    out = pl.pallas_call(
        _attn_kernel,
        grid=(batch, seq // SEGMENT, ATT_GROUP_HEADS),
        in_specs=in_specs,
        out_specs=pl.BlockSpec((1, SEGMENT, HEAD_DIM), lambda b, s, hh: (b, s, hh)),
        out_shape=jax.ShapeDtypeStruct((batch, seq, ATT_GROUP_WIDTH), BF16),
        scratch_shapes=[pltpu.VMEM((ATT_GROUPS, SEGMENT, HEAD_DIM), F32),
                        pltpu.VMEM((ATT_GROUPS, SEGMENT, HEAD_DIM), F32)],
        compiler_params=_params("parallel", "parallel", "parallel"),
    )(slopes, *([view] * (5 * ATT_GROUPS)))
    return out.reshape(batch * seq, ATT_GROUP_WIDTH)


def _causal_conv(x, tail_ref, w_ref, b_ref):
    rows = x.shape[0]
    ext = jnp.concatenate([tail_ref[...], x], axis=0)
    y = w_ref[CONV_WIDTH - 1:CONV_WIDTH, :] * x + b_ref[...]
    for back in range(1, CONV_WIDTH):
        shifted = pltpu.roll(ext, back, axis=0)[SUBLANES:SUBLANES + rows]
        y = y + w_ref[CONV_WIDTH - 1 - back:CONV_WIDTH - back, :] * shifted
    tail_ref[...] = x[rows - SUBLANES:rows]
    return y


def _lru_kernel(x_ref, g_ref, cw_ref, cb_ref, wa_ref, ba_ref, wx_ref, bx_ref, lam_ref, o_ref,
                tail_ref, h_ref):
    @pl.when(pl.program_id(2) == 0)
    def _():
        tail_ref[...] = jnp.zeros_like(tail_ref)
        h_ref[...] = jnp.zeros_like(h_ref)

    rows, width = x_ref.shape[1], x_ref.shape[2]
    xc = _causal_conv(x_ref[0], tail_ref, cw_ref, cb_ref)
    xb = xc.astype(BF16)
    pre_a, pre_x = [], []
    for blk in range(width // LRU_BLOCK):
        sl = slice(blk * LRU_BLOCK, (blk + 1) * LRU_BLOCK)
        pre_a.append(_dot(xb[:, sl], wa_ref[blk]))
        pre_x.append(_dot(xb[:, sl], wx_ref[blk]))
    r = jax.nn.sigmoid(jnp.concatenate(pre_a, axis=1) + ba_ref[...])
    i = jax.nn.sigmoid(jnp.concatenate(pre_x, axis=1) + bx_ref[...])
    log_a = -LRU_C * r * _softplus(-lam_ref[...])
    a = jnp.exp(log_a)
    b = jnp.sqrt(1.0 - a * a) * (i * xc)
    row = lax.broadcasted_iota(jnp.int32, (rows, width), 0) & (SUBLANES - 1)
    step = 1
    while step < SUBLANES:
        keep = row >= step
        a_prev = pltpu.roll(a, step, axis=0)
        b_prev = pltpu.roll(b, step, axis=0)
        b = jnp.where(keep, a * b_prev + b, b)
        a = jnp.where(keep, a * a_prev, a)
        step *= 2
    carry = h_ref[...]
    hs = []
    for grp in range(rows // SUBLANES):
        sl = slice(grp * SUBLANES, (grp + 1) * SUBLANES)
        h_grp = a[sl] * carry + b[sl]
        hs.append(h_grp)
        carry = h_grp[SUBLANES - 1:SUBLANES]
    h_ref[...] = carry
    h = jnp.concatenate(hs, axis=0)
    o_ref[0] = (jax.nn.gelu(g_ref[0]) * h).astype(o_ref.dtype)


def _lru_branch(proj, batch, seq, x_col, g_col, width, p, rows=256, cols=512):
    rows, cols = min(rows, seq), min(cols, width)
    view = proj.reshape(batch, seq, proj.shape[1])
    nblk = cols // LRU_BLOCK
    vec = lambda: pl.BlockSpec((1, cols), lambda b, c, t: (0, c))
    out = pl.pallas_call(
        _lru_kernel,
        grid=(batch, width // cols, seq // rows),
        in_specs=[pl.BlockSpec((1, rows, cols), lambda b, c, t: (b, t, x_col // cols + c)),
                  pl.BlockSpec((1, rows, cols), lambda b, c, t: (b, t, g_col // cols + c)),
                  pl.BlockSpec((CONV_WIDTH, cols), lambda b, c, t: (0, c)),
                  vec(),
                  pl.BlockSpec((nblk, LRU_BLOCK, LRU_BLOCK), lambda b, c, t: (c, 0, 0)),
                  vec(),
                  pl.BlockSpec((nblk, LRU_BLOCK, LRU_BLOCK), lambda b, c, t: (c, 0, 0)),
                  vec(), vec()],
        out_specs=pl.BlockSpec((1, rows, cols), lambda b, c, t: (b, t, c)),
        out_shape=jax.ShapeDtypeStruct((batch, seq, width), BF16),
        scratch_shapes=[pltpu.VMEM((SUBLANES, cols), F32), pltpu.VMEM((1, cols), F32)],
        compiler_params=_params("parallel", "parallel", "arbitrary"),
    )(view, view, p["conv_w"], p["conv_b"].reshape(1, width), p["w_a"], p["b_a"].reshape(1, width),
      p["w_x"], p["b_x"].reshape(1, width), p["lam"].reshape(1, width))
    return out.reshape(batch * seq, width)


def _ssd_kernel(x_ref, bc_ref, z_ref, dt_ref, xw_ref, xb_ref, bcw_ref, bcb_ref, dtb_ref, alog_ref, dskip_ref,
                ng_ref, o_ref, xtail_ref, bctail_ref, st_ref):
    @pl.when(pl.program_id(1) == 0)
    def _():
        xtail_ref[...] = jnp.zeros_like(xtail_ref)
        bctail_ref[...] = jnp.zeros_like(bctail_ref)
        st_ref[...] = jnp.zeros_like(st_ref)

    chunk = SSD_CHUNK
    gw = SSD_GROUPS * SSD_STATE
    d_inner = x_ref.shape[2]
    group_width = d_inner // SSD_GROUPS
    pairs = group_width // LANES
    xs = _silu(_causal_conv(x_ref[0], xtail_ref, xw_ref, xb_ref))
    bc = _silu(_causal_conv(bc_ref[0], bctail_ref, bcw_ref, bcb_ref))
    bm = bc[:, :gw]
    cm = bc[:, gw:]

    dt = _softplus(dt_ref[0] + dtb_ref[...])
    da = dt * (-jnp.exp(alog_ref[...]))
    row = lax.broadcasted_iota(jnp.int32, (chunk, LANES), 0)
    col = lax.broadcasted_iota(jnp.int32, (chunk, LANES), 1)
    cs = da
    step = 1
    while step < chunk:
        cs = cs + jnp.where(row >= step, pltpu.roll(cs, step, axis=0), 0.0)
        step *= 2
    cs_t = cs.T
    dt_t = dt.T
    to_end_t = jnp.exp(cs_t[:, chunk - 1:chunk] - cs_t) * dt_t
    from_start = jnp.exp(cs)
    causal = row >= col
    left = col < SSD_HEAD_DIM

    ys = []
    for g in range(SSD_GROUPS):
        gs = slice(g * SSD_STATE, (g + 1) * SSD_STATE)
        b_g, c_g = bm[:, gs], cm[:, gs]
        cb = _dot_nt(c_g.astype(BF16), b_g.astype(BF16))
        b_t = b_g.T
        for pr in range(pairs):
            ps = slice(g * group_width + pr * LANES, g * group_width + (pr + 1) * LANES)
            x_pair = xs[:, ps]
            st_pair = st_ref[g, :, pr * LANES:(pr + 1) * LANES]
            rhs = jnp.concatenate([x_pair, st_pair], axis=0).astype(BF16)
            x_pair_b = x_pair.astype(BF16)
            y_half, st_half = [], []
            for half in range(2):
                h = (g * pairs + pr) * 2 + half
                decay = jnp.where(causal, jnp.exp(cs[:, h:h + 1] - cs_t[h:h + 1, :]), 0.0)
                m_h = decay * cb * dt_t[h:h + 1, :]
                c_h = c_g * from_start[:, h:h + 1]
                lhs = jnp.concatenate([m_h, c_h], axis=1).astype(BF16)
                y_half.append(_dot(lhs, rhs))
                local = _dot((b_t * to_end_t[h:h + 1, :]).astype(BF16), x_pair_b)
                st_half.append(from_start[chunk - 1:chunk, h:h + 1] * st_pair + local)
            ys.append(jnp.where(left, y_half[0], y_half[1]))
            st_ref[g, :, pr * LANES:(pr + 1) * LANES] = jnp.where(left, st_half[0], st_half[1])
    y = jnp.concatenate(ys, axis=1)
    y = y + xs * dskip_ref[...]
    y = y * _silu(z_ref[0])
    outs = []
    for g in range(SSD_GROUPS):
        yg = y[:, g * group_width:(g + 1) * group_width]
        ms = jnp.mean(yg * yg, axis=-1, keepdims=True)
        outs.append(yg * lax.rsqrt(ms + NORM_EPS))
    o_ref[0] = (jnp.concatenate(outs, axis=1) * ng_ref[...]).astype(o_ref.dtype)


def _ssd_branch(proj, dt_proj, dt_col, batch, seq, z_col, xbc_col, d_inner, p):
    bc_width = 2 * SSD_GROUPS * SSD_STATE
    bc_col = xbc_col + d_inner
    heads = d_inner // SSD_HEAD_DIM
    view = proj.reshape(batch, seq, proj.shape[1])
    dt_view = dt_proj.reshape(batch, seq, dt_proj.shape[1])
    pad = lambda v: jnp.pad(v, (0, LANES - heads)).reshape(1, LANES)
    const = lambda shape: pl.BlockSpec(shape, lambda b, c: (0, 0))
    conv_w, conv_b = p["conv_w"], p["conv_b"].reshape(1, d_inner + bc_width)
    out = pl.pallas_call(
        _ssd_kernel,
        grid=(batch, seq // SSD_CHUNK),
        in_specs=[pl.BlockSpec((1, SSD_CHUNK, d_inner), lambda b, c: (b, c, xbc_col // d_inner)),
                  pl.BlockSpec((1, SSD_CHUNK, bc_width), lambda b, c: (b, c, bc_col // bc_width)),
                  pl.BlockSpec((1, SSD_CHUNK, d_inner), lambda b, c: (b, c, z_col // d_inner)),
                  pl.BlockSpec((1, SSD_CHUNK, LANES), lambda b, c: (b, c, dt_col // LANES)),
                  const((CONV_WIDTH, d_inner)), const((1, d_inner)),
                  const((CONV_WIDTH, bc_width)), const((1, bc_width)),
                  const((1, LANES)), const((1, LANES)), const((1, d_inner)), const((1, d_inner))],
        out_specs=pl.BlockSpec((1, SSD_CHUNK, d_inner), lambda b, c: (b, c, 0)),
        out_shape=jax.ShapeDtypeStruct((batch, seq, d_inner), BF16),
        scratch_shapes=[pltpu.VMEM((SUBLANES, d_inner), F32), pltpu.VMEM((SUBLANES, bc_width), F32),
                        pltpu.VMEM((SSD_GROUPS, SSD_STATE, d_inner // SSD_GROUPS), F32)],
        compiler_params=_params("parallel", "arbitrary"),
    )(view, view, view, dt_view, conv_w[:, :d_inner], conv_b[:, :d_inner], conv_w[:, d_inner:],
      conv_b[:, d_inner:], pad(p["dt_bias"]), pad(p["a_log"]),
      jnp.repeat(p["d_skip"], SSD_HEAD_DIM).reshape(1, d_inner), p["norm"].reshape(1, d_inner))
    return out.reshape(batch * seq, d_inner)


def _merge_kernel(gl_ref, ya_ref, yl_ref, ys_ref, gu0_ref, gu1_ref, gu2_ref, gb0_ref, gb1_ref, gb2_ref,
                  pa_ref, pl_ref, ps_ref, o_ref):
    g_low = gl_ref[...].astype(BF16)

    def gated(gu_ref, gb_ref, y_ref, p_ref):
        return jax.nn.sigmoid(_dot(g_low, gu_ref[...]) + gb_ref[...]) * _dot(y_ref[...], p_ref[...])

    y = (gated(gu0_ref, gb0_ref, ya_ref, pa_ref) + gated(gu1_ref, gb1_ref, yl_ref, pl_ref)
         + gated(gu2_ref, gb2_ref, ys_ref, ps_ref))
    o_ref[...] = y.astype(o_ref.dtype)


def _gated_merge(g_low, y_attn, y_lru, y_ssd, gate_up, gate_b, p_attn, p_lru, p_ssd, tm=1024, tn=512):
    m = y_attn.shape[0]
    d = p_attn.shape[1]
    tm, tn = min(tm, m), min(tn, d)
    nd = d // tn
    rows = lambda width: pl.BlockSpec((tm, width), lambda i, j: (i, 0))
    gate_w = lambda br: pl.BlockSpec((GATE_RANK, tn), lambda i, j: (0, br * nd + j))
    gate_bias = lambda br: pl.BlockSpec((1, tn), lambda i, j: (0, br * nd + j))
    branch_w = lambda k: pl.BlockSpec((k, tn), lambda i, j: (0, j))
    gb = gate_b.reshape(1, 3 * d)
    return pl.pallas_call(
        _merge_kernel,
        grid=(m // tm, nd),
        in_specs=[rows(GATE_RANK), rows(y_attn.shape[1]), rows(y_lru.shape[1]),
                  rows(y_ssd.shape[1]),
                  gate_w(0), gate_w(1), gate_w(2), gate_bias(0), gate_bias(1), gate_bias(2),
                  branch_w(p_attn.shape[0]), branch_w(p_lru.shape[0]), branch_w(p_ssd.shape[0])],
        out_specs=pl.BlockSpec((tm, tn), lambda i, j: (i, j)),
        out_shape=jax.ShapeDtypeStruct((m, d), BF16),
        compiler_params=_params("parallel", "parallel"),
    )(g_low, y_attn, y_lru, y_ssd, gate_up, gate_up, gate_up, gb, gb, gb, p_attn, p_lru, p_ssd)


def _rms_normalize(x, g):
    ms = jnp.mean(x * x, axis=-1, keepdims=True)
    return x * lax.rsqrt(ms + NORM_EPS) * g


def _xattn_kernel(h_ref, g_ref, wq_ref, kv_ref, wo_ref, gn_ref, o_ref, on_ref):
    h = h_ref[0]
    hn = _rms_normalize(h, g_ref[...]).astype(BF16)
    q = _dot(hn, wq_ref[...]).astype(BF16)
    width = XATTN_HEADS * XATTN_HEAD_DIM
    scale = XATTN_HEAD_DIM ** -0.5
    outs = []
    for hh in range(XATTN_HEADS):
        sl = slice(hh * XATTN_HEAD_DIM, (hh + 1) * XATTN_HEAD_DIM)
        vsl = slice(width + hh * XATTN_HEAD_DIM, width + (hh + 1) * XATTN_HEAD_DIM)
        s = _dot_nt(q[:, sl], kv_ref[0, :, sl]) * scale
        p = jnp.exp(s - jnp.max(s, axis=1, keepdims=True))
        denom = jnp.sum(p, axis=1, keepdims=True)
        outs.append(_dot(p.astype(BF16), kv_ref[0, :, vsl]) / denom)
    o = jnp.concatenate(outs, axis=1).astype(BF16)
    h_new = h + _dot(o, wo_ref[...])
    o_ref[0] = h_new
    on_ref[0] = _rms_normalize(h_new, gn_ref[...]).astype(on_ref.dtype)


def _memory_cross_attention(h, batch, seq, norm_g, w_q, kv, w_o, next_gain, rows=256):
    d = h.shape[1]
    rows = min(rows, seq)
    width = XATTN_HEADS * XATTN_HEAD_DIM
    mem_len = kv.shape[0] // batch
    tile = pl.BlockSpec((1, rows, d), lambda b, i: (b, i, 0))
    gain = pl.BlockSpec((1, d), lambda b, i: (0, 0))
    out, normed = pl.pallas_call(
        _xattn_kernel,
        grid=(batch, seq // rows),
        in_specs=[tile, gain,
                  pl.BlockSpec((d, width), lambda b, i: (0, 0)),
                  pl.BlockSpec((1, mem_len, 2 * width), lambda b, i: (b, 0, 0)),
                  pl.BlockSpec((width, d), lambda b, i: (0, 0)),
                  gain],
        out_specs=[tile, tile],
        out_shape=[jax.ShapeDtypeStruct((batch, seq, d), F32), jax.ShapeDtypeStruct((batch, seq, d), BF16)],
        compiler_params=_params("parallel", "parallel"),
    )(h.reshape(batch, seq, d), norm_g.reshape(1, d), w_q, kv.reshape(batch, mem_len, 2 * width), w_o,
      next_gain.reshape(1, d))
    return out.reshape(batch * seq, d), normed.reshape(batch * seq, d)


def _layer(h, hg, ssq, mem, p, next_gain, batch, seq):
    lru_width = p["lru_conv_w"].shape[1]
    d_inner = p["ssd_norm"].shape[0]
    conv_dim = p["ssd_conv_w"].shape[1]
    heads = p["ssd_dt_bias"].shape[0]

    hidden = _mm_swiglu(hg, p["ffn1_w_gu"].astype(BF16), ssq)
    h, ug, ussq = _mm_residual(hidden, p["ffn1_w_down"].astype(BF16), h, 0.5, p["mix_norm"])

    w_in = p["w_in"]
    c_lru = 3 * ATT_WIDTH
    c_dt = c_lru + 2 * lru_width + d_inner + conv_dim
    c_gate = c_dt + heads
    w_small = jnp.concatenate([w_in[:, c_gate:], w_in[:, c_dt:c_gate]], axis=1)
    w_small = jnp.pad(w_small, ((0, 0), (0, LANES - heads))).astype(BF16)
    qkv = _mm(ug, w_in[:, :c_lru].astype(BF16), F32, ussq, tn=768)
    proj = _mm(ug, w_in[:, c_lru:c_dt].astype(BF16), F32, ussq)
    small = _mm(ug, w_small, F32, ussq)

    y_attn = _dilated_attention(qkv, batch, seq)
    y_lru = _lru_branch(proj, batch, seq, 0, lru_width, lru_width,
                        dict(conv_w=p["lru_conv_w"], conv_b=p["lru_conv_b"], w_a=p["lru_w_a"].astype(BF16),
                             b_a=p["lru_b_a"], w_x=p["lru_w_x"].astype(BF16), b_x=p["lru_b_x"],
                             lam=p["lru_lambda"]))
    y_ssd = _ssd_branch(proj, small, GATE_RANK, batch, seq, 2 * lru_width, 2 * lru_width + d_inner, d_inner,
                        dict(conv_w=p["ssd_conv_w"], conv_b=p["ssd_conv_b"], dt_bias=p["ssd_dt_bias"],
                             a_log=p["ssd_a_log"], d_skip=p["ssd_d"], norm=p["ssd_norm"]))
    merged = _gated_merge(small, y_attn, y_lru, y_ssd, p["gate_up"].astype(BF16), p["gate_b"],
                          p["w_br_attn"].astype(BF16), p["w_br_lru"].astype(BF16), p["w_br_ssd"].astype(BF16))
    h = _mm_residual(merged, p["w_out"].astype(BF16), h, 1.0)

    memn = _rmsnorm(mem, p["mem_norm"], BF16)
    kv = _mm(memn, p["xattn_w_kv"].astype(BF16), BF16)
    h, hn = _memory_cross_attention(h, batch, seq, p["xattn_norm"], p["xattn_w_q"].astype(BF16), kv,
                                    p["xattn_w_o"].astype(BF16), p["ffn2_norm"])

    hidden = _mm_swiglu(hn, p["ffn2_w_gu"].astype(BF16))
    if next_gain is None:
        return _mm_residual(hidden, p["ffn2_w_down"].astype(BF16), h, 0.5), None, None
    return _mm_residual(hidden, p["ffn2_w_down"].astype(BF16), h, 0.5, next_gain)


def kernel(x, mem, ffn1_norm, ffn1_w_gu, ffn1_w_down, mix_norm, w_in, lru_conv_w, lru_conv_b, lru_w_a, lru_b_a, lru_w_x, lru_b_x, lru_lambda, ssd_conv_w, ssd_conv_b, ssd_dt_bias, ssd_a_log, ssd_d, ssd_norm, gate_up, gate_b, w_br_attn, w_br_lru, w_br_ssd, w_out, xattn_norm, mem_norm, xattn_w_q, xattn_w_kv, xattn_w_o, ffn2_norm, ffn2_w_gu, ffn2_w_down, final_norm):
    batch, seq, d = x.shape
    stacked = dict(
        ffn1_norm=ffn1_norm, ffn1_w_gu=ffn1_w_gu, ffn1_w_down=ffn1_w_down, mix_norm=mix_norm, w_in=w_in,
        lru_conv_w=lru_conv_w, lru_conv_b=lru_conv_b, lru_w_a=lru_w_a, lru_b_a=lru_b_a, lru_w_x=lru_w_x,
        lru_b_x=lru_b_x, lru_lambda=lru_lambda, ssd_conv_w=ssd_conv_w, ssd_conv_b=ssd_conv_b,
        ssd_dt_bias=ssd_dt_bias, ssd_a_log=ssd_a_log, ssd_d=ssd_d, ssd_norm=ssd_norm, gate_up=gate_up,
        gate_b=gate_b, w_br_attn=w_br_attn, w_br_lru=w_br_lru, w_br_ssd=w_br_ssd, w_out=w_out,
        xattn_norm=xattn_norm, mem_norm=mem_norm, xattn_w_q=xattn_w_q, xattn_w_kv=xattn_w_kv,
        xattn_w_o=xattn_w_o, ffn2_norm=ffn2_norm, ffn2_w_gu=ffn2_w_gu, ffn2_w_down=ffn2_w_down)
    depth = w_in.shape[0]
    mem2 = mem.reshape(batch * mem.shape[1], d)
    h = x.reshape(batch * seq, d)
    hg, ssq = _gain_ssq(h, ffn1_norm[0])
    for layer in range(depth):
        p = {name: value[layer] for name, value in stacked.items()}
        next_gain = ffn1_norm[layer + 1] if layer + 1 < depth else None
        h, hg, ssq = _layer(h, hg, ssq, mem2, p, next_gain, batch, seq)
    return _rmsnorm(h, final_norm, F32).reshape(batch, seq, d)
```

```python
import functools

import jax
import jax.numpy as jnp
from jax import lax
from jax.experimental import pallas as pl
from jax.experimental.pallas import tpu as pltpu

F32 = jnp.float32
BF16 = jnp.bfloat16

HEAD_DIM = 128
ATT_PATTERNS = ((128, 1), (512, 4), (2048, 16))
ATT_GROUPS = 3
ATT_GROUP_HEADS = 4
ATT_HEADS = ATT_GROUPS * ATT_GROUP_HEADS
ATT_GROUP_WIDTH = ATT_GROUP_HEADS * HEAD_DIM
ATT_WIDTH = ATT_HEADS * HEAD_DIM
BAND = 128
SEGMENT = 2048
LRU_BLOCK = 128
LRU_C = 8.0
CONV_WIDTH = 4
SSD_HEAD_DIM = 64
SSD_GROUPS = 4
SSD_STATE = 128
SSD_CHUNK = 128
GATE_RANK = 512
XATTN_HEADS = 4
XATTN_HEAD_DIM = 128
NORM_EPS = 1e-6

LANES = 128
SUBLANES = 8
VMEM_LIMIT_BYTES = 56 * 1024 * 1024


def _params(*semantics):
    return pltpu.CompilerParams(dimension_semantics=semantics, vmem_limit_bytes=VMEM_LIMIT_BYTES)


def _softplus(x):
    return jnp.maximum(x, 0.0) + jnp.log1p(jnp.exp(-jnp.abs(x)))


def _silu(x):
    return x * jax.nn.sigmoid(x)


def _dot(a, b):
    return jnp.dot(a, b, preferred_element_type=F32)


def _dot_nt(a, b):
    return lax.dot_general(a, b, (((1,), (1,)), ((), ())), preferred_element_type=F32)


def _rmsnorm_kernel(x_ref, g_ref, o_ref):
    x = x_ref[...]
    ms = jnp.mean(x * x, axis=-1, keepdims=True)
    o_ref[...] = (x * lax.rsqrt(ms + NORM_EPS) * g_ref[...]).astype(o_ref.dtype)


def _rmsnorm(x, g, out_dtype, rows=256):
    t, d = x.shape
    rows = min(rows, t)
    return pl.pallas_call(
        _rmsnorm_kernel,
        grid=(t // rows,),
        in_specs=[pl.BlockSpec((rows, d), lambda i: (i, 0)),
                  pl.BlockSpec((1, d), lambda i: (0, 0))],
        out_specs=pl.BlockSpec((rows, d), lambda i: (i, 0)),
        out_shape=jax.ShapeDtypeStruct((t, d), out_dtype),
        compiler_params=_params("parallel"),
    )(x, g.reshape(1, d))


def _row_rsqrt(ssq_ref, inv_d):
    return lax.rsqrt(ssq_ref[:, :1] * inv_d + NORM_EPS)


def _mm_kernel(*refs, inv_d):
    if inv_d is None:
        x_ref, w_ref, o_ref = refs
        o_ref[...] = _dot(x_ref[...], w_ref[...]).astype(o_ref.dtype)
    else:
        x_ref, w_ref, ssq_ref, o_ref = refs
        o_ref[...] = (_dot(x_ref[...], w_ref[...]) * _row_rsqrt(ssq_ref, inv_d)).astype(o_ref.dtype)


def _weight_tile(stack, layer, tn, col=lambda j: j):
    return pl.BlockSpec((None, stack.shape[1], tn), lambda i, j: (layer, 0, col(j)))


def _mm(x, w, layer, out_dtype, ssq=None, tm=1024, tn=1024):
    m, k = x.shape
    n = w.shape[2]
    tm, tn = min(tm, m), min(tn, n)
    in_specs = [pl.BlockSpec((tm, k), lambda i, j: (i, 0)), _weight_tile(w, layer, tn)]
    args = [x, w]
    if ssq is not None:
        in_specs.append(pl.BlockSpec((tm, LANES), lambda i, j: (i, 0)))
        args.append(ssq)
    return pl.pallas_call(
        functools.partial(_mm_kernel, inv_d=None if ssq is None else 1.0 / k),
        grid=(m // tm, n // tn),
        in_specs=in_specs,
        out_specs=pl.BlockSpec((tm, tn), lambda i, j: (i, j)),
        out_shape=jax.ShapeDtypeStruct((m, n), out_dtype),
        compiler_params=_params("parallel", "parallel"),
    )(*args)


def _mm_swiglu_kernel(*refs, inv_d):
    if inv_d is None:
        x_ref, wg_ref, wu_ref, o_ref = refs
        scale = None
    else:
        x_ref, wg_ref, wu_ref, ssq_ref, o_ref = refs
        scale = _row_rsqrt(ssq_ref, inv_d)
    x = x_ref[...]
    gate = _dot(x, wg_ref[...])
    up = _dot(x, wu_ref[...])
    if scale is not None:
        gate, up = gate * scale, up * scale
    o_ref[...] = (_silu(gate) * up).astype(o_ref.dtype)


def _mm_swiglu(x, w_gu, layer, ssq=None, tm=1024, tn=512):
    m, k = x.shape
    f = w_gu.shape[2] // 2
    tm, tn = min(tm, m), min(tn, f)
    nf = f // tn
    in_specs = [pl.BlockSpec((tm, k), lambda i, j: (i, 0)),
                _weight_tile(w_gu, layer, tn),
                _weight_tile(w_gu, layer, tn, lambda j: j + nf)]
    args = [x, w_gu, w_gu]
    if ssq is not None:
        in_specs.append(pl.BlockSpec((tm, LANES), lambda i, j: (i, 0)))
        args.append(ssq)
    return pl.pallas_call(
        functools.partial(_mm_swiglu_kernel, inv_d=None if ssq is None else 1.0 / k),
        grid=(m // tm, nf),
        in_specs=in_specs,
        out_specs=pl.BlockSpec((tm, tn), lambda i, j: (i, j)),
        out_shape=jax.ShapeDtypeStruct((m, f), BF16),
        compiler_params=_params("parallel", "parallel"),
    )(*args)


def _mm_residual_kernel(*refs, scale, emit):
    if emit:
        x_ref, w_ref, r_ref, g_ref, o_ref, hg_ref, ssq_ref = refs
    else:
        x_ref, w_ref, r_ref, o_ref = refs
    h = r_ref[...] + scale * _dot(x_ref[...], w_ref[...])
    o_ref[...] = h
    if emit:
        hg_ref[...] = (h * g_ref[...]).astype(hg_ref.dtype)
        part = jnp.broadcast_to(jnp.sum(h * h, axis=1, keepdims=True), ssq_ref.shape)

        @pl.when(pl.program_id(1) == 0)
        def _():
            ssq_ref[...] = part

        @pl.when(pl.program_id(1) > 0)
        def _():
            ssq_ref[...] += part


def _mm_residual(x, w, layer, res, scale, next_gain=None, tm=1024, tn=512):
    m, k = x.shape
    n = w.shape[2]
    tm, tn = min(tm, m), min(tn, n)
    emit = next_gain is not None
    tile = pl.BlockSpec((tm, tn), lambda i, j: (i, j))
    in_specs = [pl.BlockSpec((tm, k), lambda i, j: (i, 0)), _weight_tile(w, layer, tn), tile]
    args = [x, w, res]
    out_specs, out_shape = tile, jax.ShapeDtypeStruct((m, n), F32)
    if emit:
        in_specs.append(pl.BlockSpec((1, tn), lambda i, j: (0, j)))
        args.append(next_gain.reshape(1, n))
        out_specs = [tile, tile, pl.BlockSpec((tm, LANES), lambda i, j: (i, 0))]
        out_shape = [out_shape, jax.ShapeDtypeStruct((m, n), BF16), jax.ShapeDtypeStruct((m, LANES), F32)]
    return pl.pallas_call(
        functools.partial(_mm_residual_kernel, scale=scale, emit=emit),
        grid=(m // tm, n // tn),
        in_specs=in_specs,
        out_specs=out_specs,
        out_shape=out_shape,
        compiler_params=_params("parallel", "arbitrary"),
    )(*args)


def _gain_ssq_kernel(x_ref, g_ref, hg_ref, ssq_ref):
    x = x_ref[...]
    hg_ref[...] = (x * g_ref[...]).astype(hg_ref.dtype)
    ssq_ref[...] = jnp.broadcast_to(jnp.sum(x * x, axis=1, keepdims=True), ssq_ref.shape)


def _gain_ssq(x, g, rows=256):
    t, d = x.shape
    rows = min(rows, t)
    return pl.pallas_call(
        _gain_ssq_kernel,
        grid=(t // rows,),
        in_specs=[pl.BlockSpec((rows, d), lambda i: (i, 0)), pl.BlockSpec((1, d), lambda i: (0, 0))],
        out_specs=[pl.BlockSpec((rows, d), lambda i: (i, 0)), pl.BlockSpec((rows, LANES), lambda i: (i, 0))],
        out_shape=[jax.ShapeDtypeStruct((t, d), BF16), jax.ShapeDtypeStruct((t, LANES), F32)],
        compiler_params=_params("parallel"),
    )(x, g.reshape(1, d))


def _rows(start, dilation):
    return pl.ds(start, BAND, stride=dilation) if dilation > 1 else pl.ds(start, BAND)


def _attn_kernel(slope_ref, *refs):
    ins = refs[:5 * ATT_GROUPS]
    y_ref, o_scr, l_scr = refs[5 * ATT_GROUPS:]
    first_segment = pl.program_id(1) == 0
    row = lax.broadcasted_iota(jnp.int32, (BAND, BAND), 0)
    col = lax.broadcasted_iota(jnp.int32, (BAND, BAND), 1)
    lag_c = row - col
    lag_p = lag_c + BAND
    valid_c = lag_c >= 0
    valid_p_any = lag_p <= BAND
    valid_p_first = lag_p <= jnp.where(first_segment, -1, BAND)
    scale = HEAD_DIM ** -0.5
    for g, (window, dilation) in enumerate(ATT_PATTERNS):
        q_ref, kp_ref, kc_ref, vp_ref, vc_ref = ins[5 * g:5 * g + 5]
        slope = slope_ref[0, :, g:g + 1] * float(dilation)
        bias_c = slope * lag_c.astype(F32)
        bias_p = slope * lag_p.astype(F32)
        span = BAND * dilation
        for r in range(dilation):
            k_prev = kp_ref[0, _rows(r, dilation), :].astype(BF16)
            v_prev = vp_ref[0, _rows(r, dilation), :].astype(BF16)
            for blk in range(SEGMENT // span):
                rows = _rows(blk * span + r, dilation)
                q = q_ref[0, rows, :].astype(BF16)
                k_own = kc_ref[0, rows, :].astype(BF16)
                v_own = vc_ref[0, rows, :].astype(BF16)
                s_c = jnp.where(valid_c, _dot_nt(q, k_own) * scale - bias_c, -jnp.inf)
                s_p = jnp.where(valid_p_first if blk == 0 else valid_p_any,
                                _dot_nt(q, k_prev) * scale - bias_p, -jnp.inf)
                m = jnp.max(jnp.maximum(s_c, s_p), axis=1, keepdims=True)
                p_c = jnp.exp(s_c - m)
                p_p = jnp.exp(s_p - m)
                denom = jnp.sum(p_c + p_p, axis=1, keepdims=True)
                out = _dot(p_c.astype(BF16), v_own) + _dot(p_p.astype(BF16), v_prev)
                o_scr[g, rows, :] = out / denom
                l_scr[g, rows, :] = jnp.broadcast_to(m + jnp.log(denom), (BAND, HEAD_DIM))
                k_prev, v_prev = k_own, v_own
    l0, l1, l2 = l_scr[0], l_scr[1], l_scr[2]
    m = jnp.maximum(jnp.maximum(l0, l1), l2)
    e0, e1, e2 = jnp.exp(l0 - m), jnp.exp(l1 - m), jnp.exp(l2 - m)
    y = (e0 * o_scr[0] + e1 * o_scr[1] + e2 * o_scr[2]) / (e0 + e1 + e2)
    y_ref[0] = y.astype(y_ref.dtype)


def _dilated_attention(qkv, batch, seq):
    view = qkv.reshape(batch, seq, 3 * ATT_WIDTH)
    heads = ATT_WIDTH // HEAD_DIM
    slopes = jnp.asarray(
        [[[2.0 ** (-8.0 * (g * ATT_GROUP_HEADS + hh + 1) / ATT_HEADS) if g < ATT_GROUPS else 0.0
           for g in range(LANES)]] for hh in range(ATT_GROUP_HEADS)], F32)
    in_specs = [pl.BlockSpec((1, 1, LANES), lambda b, s, hh: (hh, 0, 0))]
    for g, (window, dilation) in enumerate(ATT_PATTERNS):
        assert window // dilation == BAND and SEGMENT % (BAND * dilation) == 0
        span = BAND * dilation
        per_seg = SEGMENT // span

        def cur(which, g=g):
            return pl.BlockSpec((1, SEGMENT, HEAD_DIM),
                                lambda b, s, hh: (b, s, which * heads + g * ATT_GROUP_HEADS + hh))

        def prev(which, g=g, span=span, per_seg=per_seg):
            return pl.BlockSpec((1, span, HEAD_DIM),
                                lambda b, s, hh: (b, jnp.maximum(s * per_seg - 1, 0),
                                                  which * heads + g * ATT_GROUP_HEADS + hh))

        in_specs += [cur(0), prev(1), cur(1), prev(2), cur(2)]
    out = pl.pallas_call(
        _attn_kernel,
        grid=(batch, seq // SEGMENT, ATT_GROUP_HEADS),
        in_specs=in_specs,
        out_specs=pl.BlockSpec((1, SEGMENT, HEAD_DIM), lambda b, s, hh: (b, s, hh)),
        out_shape=jax.ShapeDtypeStruct((batch, seq, ATT_GROUP_WIDTH), BF16),
        scratch_shapes=[pltpu.VMEM((ATT_GROUPS, SEGMENT, HEAD_DIM), F32),
                        pltpu.VMEM((ATT_GROUPS, SEGMENT, HEAD_DIM), F32)],
        compiler_params=_params("parallel", "parallel", "parallel"),
    )(slopes, *([view] * (5 * ATT_GROUPS)))
    return out.reshape(batch * seq, ATT_GROUP_WIDTH)


def _causal_conv(x, tail_ref, w_ref, b_ref):
    rows = x.shape[0]
    ext = jnp.concatenate([tail_ref[...], x], axis=0)
    y = w_ref[CONV_WIDTH - 1:CONV_WIDTH, :] * x + b_ref[...]
    for back in range(1, CONV_WIDTH):
        shifted = pltpu.roll(ext, back, axis=0)[SUBLANES:SUBLANES + rows]
        y = y + w_ref[CONV_WIDTH - 1 - back:CONV_WIDTH - back, :] * shifted
    tail_ref[...] = x[rows - SUBLANES:rows]
    return y


def _lru_kernel(x_ref, g_ref, cw_ref, cb_ref, wa_ref, ba_ref, wx_ref, bx_ref, lam_ref, o_ref,
                tail_ref, h_ref):
    @pl.when(pl.program_id(2) == 0)
    def _():
        tail_ref[...] = jnp.zeros_like(tail_ref)
        h_ref[...] = jnp.zeros_like(h_ref)

    rows, width = x_ref.shape[1], x_ref.shape[2]
    xc = _causal_conv(x_ref[0], tail_ref, cw_ref, cb_ref)
    xb = xc.astype(BF16)
    pre_a, pre_x = [], []
    for blk in range(width // LRU_BLOCK):
        sl = slice(blk * LRU_BLOCK, (blk + 1) * LRU_BLOCK)
        pre_a.append(_dot(xb[:, sl], wa_ref[blk]))
        pre_x.append(_dot(xb[:, sl], wx_ref[blk]))
    r = jax.nn.sigmoid(jnp.concatenate(pre_a, axis=1) + ba_ref[...])
    i = jax.nn.sigmoid(jnp.concatenate(pre_x, axis=1) + bx_ref[...])
    log_a = -LRU_C * r * _softplus(-lam_ref[...])
    a = jnp.exp(log_a)
    one_minus_a2 = 1.0 - a * a
    b = jnp.where(one_minus_a2 > 0.0, one_minus_a2 * lax.rsqrt(one_minus_a2), 0.0) * (i * xc)
    sub = lax.broadcasted_iota(jnp.int32, (SUBLANES, width), 0)
    steps = [(step, sub >= step) for step in (1, 2, 4)]
    carry = h_ref[...]
    hs = []
    for grp in range(rows // SUBLANES):
        sl = slice(grp * SUBLANES, (grp + 1) * SUBLANES)
        a_grp, b_grp = a[sl], b[sl]
        for step, keep in steps:
            a_prev = pltpu.roll(a_grp, step, axis=0)
            b_prev = pltpu.roll(b_grp, step, axis=0)
            b_grp = jnp.where(keep, a_grp * b_prev + b_grp, b_grp)
            a_grp = jnp.where(keep, a_grp * a_prev, a_grp)
        h_grp = a_grp * carry + b_grp
        hs.append(h_grp)
        carry = h_grp[SUBLANES - 1:SUBLANES]
    h_ref[...] = carry
    h = jnp.concatenate(hs, axis=0)
    o_ref[0] = (jax.nn.gelu(g_ref[0]) * h).astype(o_ref.dtype)


def _lru_branch(proj, batch, seq, x_col, g_col, width, p, layer, rows=256, cols=512):
    rows, cols = min(rows, seq), min(cols, width)
    view = proj.reshape(batch, seq, proj.shape[1])
    nblk = cols // LRU_BLOCK
    vec = lambda: pl.BlockSpec((1, cols), lambda b, c, t: (0, c))
    gate_w = lambda: pl.BlockSpec((None, nblk, LRU_BLOCK, LRU_BLOCK), lambda b, c, t: (layer, c, 0, 0))
    out = pl.pallas_call(
        _lru_kernel,
        grid=(batch, width // cols, seq // rows),
        in_specs=[pl.BlockSpec((1, rows, cols), lambda b, c, t: (b, t, x_col // cols + c)),
                  pl.BlockSpec((1, rows, cols), lambda b, c, t: (b, t, g_col // cols + c)),
                  pl.BlockSpec((CONV_WIDTH, cols), lambda b, c, t: (0, c)),
                  vec(), gate_w(), vec(), gate_w(), vec(), vec()],
        out_specs=pl.BlockSpec((1, rows, cols), lambda b, c, t: (b, t, c)),
        out_shape=jax.ShapeDtypeStruct((batch, seq, width), BF16),
        scratch_shapes=[pltpu.VMEM((SUBLANES, cols), F32), pltpu.VMEM((1, cols), F32)],
        compiler_params=_params("parallel", "parallel", "arbitrary"),
    )(view, view, p["conv_w"], p["conv_b"].reshape(1, width), p["w_a"], p["b_a"].reshape(1, width),
      p["w_x"], p["b_x"].reshape(1, width), p["lam"].reshape(1, width))
    return out.reshape(batch * seq, width)


def _ssd_kernel(x_ref, bc_ref, z_ref, dt_ref, xw_ref, xb_ref, bcw_ref, bcb_ref, dtb_ref, alog_ref, dskip_ref,
                ng_ref, o_ref, xtail_ref, bctail_ref, st_ref):
    @pl.when(pl.program_id(1) == 0)
    def _():
        xtail_ref[...] = jnp.zeros_like(xtail_ref)
        bctail_ref[...] = jnp.zeros_like(bctail_ref)
        st_ref[...] = jnp.zeros_like(st_ref)

    chunk = SSD_CHUNK
    gw = SSD_GROUPS * SSD_STATE
    d_inner = x_ref.shape[2]
    group_width = d_inner // SSD_GROUPS
    pairs = group_width // LANES
    xs = _silu(_causal_conv(x_ref[0], xtail_ref, xw_ref, xb_ref))
    bc = _silu(_causal_conv(bc_ref[0], bctail_ref, bcw_ref, bcb_ref))
    bm = bc[:, :gw]
    cm = bc[:, gw:]

    dt = _softplus(dt_ref[0] + dtb_ref[...])
    da = dt * (-jnp.exp(alog_ref[...]))
    row = lax.broadcasted_iota(jnp.int32, (chunk, LANES), 0)
    col = lax.broadcasted_iota(jnp.int32, (chunk, LANES), 1)
    cs = da
    step = 1
    while step < chunk:
        cs = cs + jnp.where(row >= step, pltpu.roll(cs, step, axis=0), 0.0)
        step *= 2
    cs_t = cs.T
    dt_t = dt.T
    to_end_t = jnp.exp(cs_t[:, chunk - 1:chunk] - cs_t) * dt_t
    from_start = jnp.exp(cs)
    causal = row >= col
    left = col < SSD_HEAD_DIM

    ys = []
    for g in range(SSD_GROUPS):
        gs = slice(g * SSD_STATE, (g + 1) * SSD_STATE)
        b_g, c_g = bm[:, gs], cm[:, gs]
        cb = _dot_nt(c_g.astype(BF16), b_g.astype(BF16))
        b_t = b_g.T
        for pr in range(pairs):
            ps = slice(g * group_width + pr * LANES, g * group_width + (pr + 1) * LANES)
            x_pair = xs[:, ps]
            st_pair = st_ref[g, :, pr * LANES:(pr + 1) * LANES]
            rhs = jnp.concatenate([x_pair, st_pair], axis=0).astype(BF16)
            x_pair_b = x_pair.astype(BF16)
            y_half, st_half = [], []
            for half in range(2):
                h = (g * pairs + pr) * 2 + half
                decay = jnp.where(causal, jnp.exp(cs[:, h:h + 1] - cs_t[h:h + 1, :]), 0.0)
                m_h = decay * cb * dt_t[h:h + 1, :]
                c_h = c_g * from_start[:, h:h + 1]
                lhs = jnp.concatenate([m_h, c_h], axis=1).astype(BF16)
                y_half.append(_dot(lhs, rhs))
                local = _dot((b_t * to_end_t[h:h + 1, :]).astype(BF16), x_pair_b)
                st_half.append(from_start[chunk - 1:chunk, h:h + 1] * st_pair + local)
            ys.append(jnp.where(left, y_half[0], y_half[1]))
            st_ref[g, :, pr * LANES:(pr + 1) * LANES] = jnp.where(left, st_half[0], st_half[1])
    y = jnp.concatenate(ys, axis=1)
    y = y + xs * dskip_ref[...]
    y = y * _silu(z_ref[0])
    outs = []
    for g in range(SSD_GROUPS):
        yg = y[:, g * group_width:(g + 1) * group_width]
        ms = jnp.mean(yg * yg, axis=-1, keepdims=True)
        outs.append(yg * lax.rsqrt(ms + NORM_EPS))
    o_ref[0] = (jnp.concatenate(outs, axis=1) * ng_ref[...]).astype(o_ref.dtype)


def _ssd_branch(proj, dt_proj, dt_col, batch, seq, z_col, xbc_col, d_inner, p):
    bc_width = 2 * SSD_GROUPS * SSD_STATE
    bc_col = xbc_col + d_inner
    heads = d_inner // SSD_HEAD_DIM
    view = proj.reshape(batch, seq, proj.shape[1])
    dt_view = dt_proj.reshape(batch, seq, dt_proj.shape[1])
    pad = lambda v: jnp.pad(v, (0, LANES - heads)).reshape(1, LANES)
    const = lambda shape: pl.BlockSpec(shape, lambda b, c: (0, 0))
    conv_w, conv_b = p["conv_w"], p["conv_b"].reshape(1, d_inner + bc_width)
    out = pl.pallas_call(
        _ssd_kernel,
        grid=(batch, seq // SSD_CHUNK),
        in_specs=[pl.BlockSpec((1, SSD_CHUNK, d_inner), lambda b, c: (b, c, xbc_col // d_inner)),
                  pl.BlockSpec((1, SSD_CHUNK, bc_width), lambda b, c: (b, c, bc_col // bc_width)),
                  pl.BlockSpec((1, SSD_CHUNK, d_inner), lambda b, c: (b, c, z_col // d_inner)),
                  pl.BlockSpec((1, SSD_CHUNK, LANES), lambda b, c: (b, c, dt_col // LANES)),
                  const((CONV_WIDTH, d_inner)), const((1, d_inner)),
                  const((CONV_WIDTH, bc_width)), const((1, bc_width)),
                  const((1, LANES)), const((1, LANES)), const((1, d_inner)), const((1, d_inner))],
        out_specs=pl.BlockSpec((1, SSD_CHUNK, d_inner), lambda b, c: (b, c, 0)),
        out_shape=jax.ShapeDtypeStruct((batch, seq, d_inner), BF16),
        scratch_shapes=[pltpu.VMEM((SUBLANES, d_inner), F32), pltpu.VMEM((SUBLANES, bc_width), F32),
                        pltpu.VMEM((SSD_GROUPS, SSD_STATE, d_inner // SSD_GROUPS), F32)],
        compiler_params=_params("parallel", "arbitrary"),
    )(view, view, view, dt_view, conv_w[:, :d_inner], conv_b[:, :d_inner], conv_w[:, d_inner:],
      conv_b[:, d_inner:], pad(p["dt_bias"]), pad(p["a_log"]),
      jnp.repeat(p["d_skip"], SSD_HEAD_DIM).reshape(1, d_inner), p["norm"].reshape(1, d_inner))
    return out.reshape(batch * seq, d_inner)


def _merge_kernel(gl_ref, ya_ref, yl_ref, ys_ref, gu0_ref, gu1_ref, gu2_ref, gb0_ref, gb1_ref, gb2_ref,
                  pa_ref, pl_ref, ps_ref, o_ref):
    g_low = gl_ref[...].astype(BF16)

    def gated(gu_ref, gb_ref, y_ref, p_ref):
        return jax.nn.sigmoid(_dot(g_low, gu_ref[...]) + gb_ref[...]) * _dot(y_ref[...], p_ref[...])

    y = (gated(gu0_ref, gb0_ref, ya_ref, pa_ref) + gated(gu1_ref, gb1_ref, yl_ref, pl_ref)
         + gated(gu2_ref, gb2_ref, ys_ref, ps_ref))
    o_ref[...] = y.astype(o_ref.dtype)


def _gated_merge(g_low, y_attn, y_lru, y_ssd, gate_up, gate_b, p_attn, p_lru, p_ssd, layer, tm=1024, tn=512):
    m = y_attn.shape[0]
    d = p_attn.shape[2]
    tm, tn = min(tm, m), min(tn, d)
    nd = d // tn
    rows = lambda width: pl.BlockSpec((tm, width), lambda i, j: (i, 0))
    gate_w = lambda br: _weight_tile(gate_up, layer, tn, lambda j: br * nd + j)
    gate_bias = lambda br: pl.BlockSpec((1, tn), lambda i, j: (0, br * nd + j))
    gb = gate_b.reshape(1, 3 * d)
    return pl.pallas_call(
        _merge_kernel,
        grid=(m // tm, nd),
        in_specs=[rows(GATE_RANK), rows(y_attn.shape[1]), rows(y_lru.shape[1]),
                  rows(y_ssd.shape[1]),
                  gate_w(0), gate_w(1), gate_w(2), gate_bias(0), gate_bias(1), gate_bias(2),
                  _weight_tile(p_attn, layer, tn), _weight_tile(p_lru, layer, tn),
                  _weight_tile(p_ssd, layer, tn)],
        out_specs=pl.BlockSpec((tm, tn), lambda i, j: (i, j)),
        out_shape=jax.ShapeDtypeStruct((m, d), BF16),
        compiler_params=_params("parallel", "parallel"),
    )(g_low, y_attn, y_lru, y_ssd, gate_up, gate_up, gate_up, gb, gb, gb, p_attn, p_lru, p_ssd)


def _rms_normalize(x, g):
    ms = jnp.mean(x * x, axis=-1, keepdims=True)
    return x * lax.rsqrt(ms + NORM_EPS) * g


def _xattn_kernel(h_ref, g_ref, wq_ref, kv_ref, wo_ref, gn_ref, o_ref, on_ref):
    h = h_ref[0]
    hn = _rms_normalize(h, g_ref[...]).astype(BF16)
    q = _dot(hn, wq_ref[...]).astype(BF16)
    width = XATTN_HEADS * XATTN_HEAD_DIM
    scale = XATTN_HEAD_DIM ** -0.5
    outs = []
    for hh in range(XATTN_HEADS):
        sl = slice(hh * XATTN_HEAD_DIM, (hh + 1) * XATTN_HEAD_DIM)
        vsl = slice(width + hh * XATTN_HEAD_DIM, width + (hh + 1) * XATTN_HEAD_DIM)
        s = _dot_nt(q[:, sl], kv_ref[0, :, sl]) * scale
        p = jnp.exp(s - jnp.max(s, axis=1, keepdims=True))
        denom = jnp.sum(p, axis=1, keepdims=True)
        outs.append(_dot(p.astype(BF16), kv_ref[0, :, vsl]) / denom)
    o = jnp.concatenate(outs, axis=1).astype(BF16)
    h_new = h + _dot(o, wo_ref[...])
    o_ref[0] = h_new
    on_ref[0] = _rms_normalize(h_new, gn_ref[...]).astype(on_ref.dtype)


def _memory_cross_attention(h, batch, seq, norm_g, w_q, kv, w_o, layer, next_gain, rows=256):
    d = h.shape[1]
    rows = min(rows, seq)
    width = XATTN_HEADS * XATTN_HEAD_DIM
    mem_len = kv.shape[0] // batch
    tile = pl.BlockSpec((1, rows, d), lambda b, i: (b, i, 0))
    gain = pl.BlockSpec((1, d), lambda b, i: (0, 0))
    out, normed = pl.pallas_call(
        _xattn_kernel,
        grid=(batch, seq // rows),
        in_specs=[tile, gain,
                  pl.BlockSpec((None, d, width), lambda b, i: (layer, 0, 0)),
                  pl.BlockSpec((1, mem_len, 2 * width), lambda b, i: (b, 0, 0)),
                  pl.BlockSpec((None, width, d), lambda b, i: (layer, 0, 0)),
                  gain],
        out_specs=[tile, tile],
        out_shape=[jax.ShapeDtypeStruct((batch, seq, d), F32), jax.ShapeDtypeStruct((batch, seq, d), BF16)],
        compiler_params=_params("parallel", "parallel"),
    )(h.reshape(batch, seq, d), norm_g.reshape(1, d), w_q, kv.reshape(batch, mem_len, 2 * width), w_o,
      next_gain.reshape(1, d))
    return out.reshape(batch * seq, d), normed.reshape(batch * seq, d)


def _layer(h, hg, ssq, mem, w, p, layer, next_gain, batch, seq):
    lru_width = p["lru_conv_w"].shape[1]
    d_inner = p["ssd_norm"].shape[0]

    hidden = _mm_swiglu(hg, w["ffn1_w_gu"], layer, ssq)
    h, ug, ussq = _mm_residual(hidden, w["ffn1_w_down"], layer, h, 0.5, p["mix_norm"])

    qkv = _mm(ug, w["in_qkv"], layer, F32, ussq, tn=768)
    proj = _mm(ug, w["in_branches"], layer, F32, ussq)
    small = _mm(ug, w["in_small"], layer, F32, ussq)

    y_attn = _dilated_attention(qkv, batch, seq)
    y_lru = _lru_branch(proj, batch, seq, 0, lru_width, lru_width,
                        dict(conv_w=p["lru_conv_w"], conv_b=p["lru_conv_b"], w_a=w["lru_w_a"], b_a=p["lru_b_a"],
                             w_x=w["lru_w_x"], b_x=p["lru_b_x"], lam=p["lru_lambda"]), layer)
    y_ssd = _ssd_branch(proj, small, GATE_RANK, batch, seq, 2 * lru_width, 2 * lru_width + d_inner, d_inner,
                        dict(conv_w=p["ssd_conv_w"], conv_b=p["ssd_conv_b"], dt_bias=p["ssd_dt_bias"],
                             a_log=p["ssd_a_log"], d_skip=p["ssd_d"], norm=p["ssd_norm"]))
    merged = _gated_merge(small, y_attn, y_lru, y_ssd, w["gate_up"], p["gate_b"],
                          w["w_br_attn"], w["w_br_lru"], w["w_br_ssd"], layer)
    h = _mm_residual(merged, w["w_out"], layer, h, 1.0)

    memn = _rmsnorm(mem, p["mem_norm"], BF16)
    kv = _mm(memn, w["xattn_w_kv"], layer, BF16)
    h, hn = _memory_cross_attention(h, batch, seq, p["xattn_norm"], w["xattn_w_q"], kv, w["xattn_w_o"], layer,
                                    p["ffn2_norm"])

    hidden = _mm_swiglu(hn, w["ffn2_w_gu"], layer)
    if next_gain is None:
        return _mm_residual(hidden, w["ffn2_w_down"], layer, h, 0.5), None, None
    return _mm_residual(hidden, w["ffn2_w_down"], layer, h, 0.5, next_gain)


def kernel(x, mem, ffn1_norm, ffn1_w_gu, ffn1_w_down, mix_norm, w_in, lru_conv_w, lru_conv_b, lru_w_a, lru_b_a, lru_w_x, lru_b_x, lru_lambda, ssd_conv_w, ssd_conv_b, ssd_dt_bias, ssd_a_log, ssd_d, ssd_norm, gate_up, gate_b, w_br_attn, w_br_lru, w_br_ssd, w_out, xattn_norm, mem_norm, xattn_w_q, xattn_w_kv, xattn_w_o, ffn2_norm, ffn2_w_gu, ffn2_w_down, final_norm):
    batch, seq, d = x.shape
    depth = w_in.shape[0]
    heads = ssd_dt_bias.shape[1]
    c_lru = 3 * ATT_WIDTH
    c_dt = c_lru + 2 * lru_conv_w.shape[2] + ssd_norm.shape[1] + ssd_conv_w.shape[2]
    c_gate = c_dt + heads
    in_small = jnp.concatenate([w_in[:, :, c_gate:], w_in[:, :, c_dt:c_gate]], axis=2)
    in_small = jnp.pad(in_small, ((0, 0), (0, 0), (0, LANES - heads)))
    weights = dict(
        ffn1_w_gu=ffn1_w_gu, ffn1_w_down=ffn1_w_down, in_qkv=w_in[:, :, :c_lru],
        in_branches=w_in[:, :, c_lru:c_dt], in_small=in_small, lru_w_a=lru_w_a, lru_w_x=lru_w_x,
        gate_up=gate_up, w_br_attn=w_br_attn, w_br_lru=w_br_lru, w_br_ssd=w_br_ssd, w_out=w_out,
        xattn_w_q=xattn_w_q, xattn_w_kv=xattn_w_kv, xattn_w_o=xattn_w_o, ffn2_w_gu=ffn2_w_gu,
        ffn2_w_down=ffn2_w_down)
    weights = {name: value.astype(BF16) for name, value in weights.items()}
    small = dict(
        mix_norm=mix_norm, lru_conv_w=lru_conv_w, lru_conv_b=lru_conv_b, lru_b_a=lru_b_a, lru_b_x=lru_b_x,
        lru_lambda=lru_lambda, ssd_conv_w=ssd_conv_w, ssd_conv_b=ssd_conv_b, ssd_dt_bias=ssd_dt_bias,
        ssd_a_log=ssd_a_log, ssd_d=ssd_d, ssd_norm=ssd_norm, gate_b=gate_b, xattn_norm=xattn_norm,
        mem_norm=mem_norm, ffn2_norm=ffn2_norm)
    mem2 = mem.reshape(batch * mem.shape[1], d)
    h = x.reshape(batch * seq, d)
    hg, ssq = _gain_ssq(h, ffn1_norm[0])
    for layer in range(depth):
        p = {name: value[layer] for name, value in small.items()}
        next_gain = ffn1_norm[layer + 1] if layer + 1 < depth else None
        h, hg, ssq = _layer(h, hg, ssq, mem2, weights, p, layer, next_gain, batch, seq)
    return _rmsnorm(h, final_norm, F32).reshape(batch, seq, d)
```

```python
import functools

import jax
import jax.numpy as jnp
from jax import lax
from jax.experimental import pallas as pl
from jax.experimental.pallas import tpu as pltpu

F32 = jnp.float32
BF16 = jnp.bfloat16

HEAD_DIM = 128
ATT_PATTERNS = ((128, 1), (512, 4), (2048, 16))
ATT_GROUPS = 3
ATT_GROUP_HEADS = 4
ATT_HEADS = ATT_GROUPS * ATT_GROUP_HEADS
ATT_GROUP_WIDTH = ATT_GROUP_HEADS * HEAD_DIM
ATT_WIDTH = ATT_HEADS * HEAD_DIM
BAND = 128
SEGMENT = 2048
LRU_BLOCK = 128
LRU_C = 8.0
CONV_WIDTH = 4
SSD_HEAD_DIM = 64
SSD_GROUPS = 4
SSD_STATE = 128
SSD_CHUNK = 128
GATE_RANK = 512
XATTN_HEADS = 4
XATTN_HEAD_DIM = 128
NORM_EPS = 1e-6

LANES = 128
SUBLANES = 8
VMEM_LIMIT_BYTES = 56 * 1024 * 1024


def _params(*semantics):
    return pltpu.CompilerParams(dimension_semantics=semantics, vmem_limit_bytes=VMEM_LIMIT_BYTES)


def _softplus(x):
    return jnp.maximum(x, 0.0) + jnp.log1p(jnp.exp(-jnp.abs(x)))


def _silu(x):
    return x * jax.nn.sigmoid(x)


def _dot(a, b):
    return jnp.dot(a, b, preferred_element_type=F32)


def _dot_nt(a, b):
    return lax.dot_general(a, b, (((1,), (1,)), ((), ())), preferred_element_type=F32)


def _rmsnorm_kernel(x_ref, g_ref, o_ref):
    x = x_ref[...]
    ms = jnp.mean(x * x, axis=-1, keepdims=True)
    o_ref[...] = (x * lax.rsqrt(ms + NORM_EPS) * g_ref[...]).astype(o_ref.dtype)


def _rmsnorm(x, g, out_dtype, rows=256):
    t, d = x.shape
    rows = min(rows, t)
    return pl.pallas_call(
        _rmsnorm_kernel,
        grid=(t // rows,),
        in_specs=[pl.BlockSpec((rows, d), lambda i: (i, 0)),
                  pl.BlockSpec((1, d), lambda i: (0, 0))],
        out_specs=pl.BlockSpec((rows, d), lambda i: (i, 0)),
        out_shape=jax.ShapeDtypeStruct((t, d), out_dtype),
        compiler_params=_params("parallel"),
    )(x, g.reshape(1, d))


def _row_rsqrt(ssq_ref, inv_d):
    return lax.rsqrt(ssq_ref[:, :1] * inv_d + NORM_EPS)


def _mm_kernel(*refs, inv_d):
    if inv_d is None:
        x_ref, w_ref, o_ref = refs
        o_ref[...] = _dot(x_ref[...], w_ref[...]).astype(o_ref.dtype)
    else:
        x_ref, w_ref, ssq_ref, o_ref = refs
        o_ref[...] = (_dot(x_ref[...], w_ref[...]) * _row_rsqrt(ssq_ref, inv_d)).astype(o_ref.dtype)


def _weight_tile(stack, layer, tn, col=lambda j: j):
    return pl.BlockSpec((None, stack.shape[1], tn), lambda i, j: (layer, 0, col(j)))


def _mm(x, w, layer, out_dtype, ssq=None, tm=1024, tn=1024):
    m, k = x.shape
    n = w.shape[2]
    tm, tn = min(tm, m), min(tn, n)
    in_specs = [pl.BlockSpec((tm, k), lambda i, j: (i, 0)), _weight_tile(w, layer, tn)]
    args = [x, w]
    if ssq is not None:
        in_specs.append(pl.BlockSpec((tm, LANES), lambda i, j: (i, 0)))
        args.append(ssq)
    return pl.pallas_call(
        functools.partial(_mm_kernel, inv_d=None if ssq is None else 1.0 / k),
        grid=(m // tm, n // tn),
        in_specs=in_specs,
        out_specs=pl.BlockSpec((tm, tn), lambda i, j: (i, j)),
        out_shape=jax.ShapeDtypeStruct((m, n), out_dtype),
        compiler_params=_params("parallel", "parallel"),
    )(*args)


def _mm_swiglu_kernel(*refs, inv_d):
    if inv_d is None:
        x_ref, wg_ref, wu_ref, o_ref = refs
        scale = None
    else:
        x_ref, wg_ref, wu_ref, ssq_ref, o_ref = refs
        scale = _row_rsqrt(ssq_ref, inv_d)
    x = x_ref[...]
    gate = _dot(x, wg_ref[...])
    up = _dot(x, wu_ref[...])
    if scale is not None:
        gate, up = gate * scale, up * scale
    o_ref[...] = (_silu(gate) * up).astype(o_ref.dtype)


def _mm_swiglu(x, w_gu, layer, ssq=None, tm=1024, tn=512):
    m, k = x.shape
    f = w_gu.shape[2] // 2
    tm, tn = min(tm, m), min(tn, f)
    nf = f // tn
    in_specs = [pl.BlockSpec((tm, k), lambda i, j: (i, 0)),
                _weight_tile(w_gu, layer, tn),
                _weight_tile(w_gu, layer, tn, lambda j: j + nf)]
    args = [x, w_gu, w_gu]
    if ssq is not None:
        in_specs.append(pl.BlockSpec((tm, LANES), lambda i, j: (i, 0)))
        args.append(ssq)
    return pl.pallas_call(
        functools.partial(_mm_swiglu_kernel, inv_d=None if ssq is None else 1.0 / k),
        grid=(m // tm, nf),
        in_specs=in_specs,
        out_specs=pl.BlockSpec((tm, tn), lambda i, j: (i, j)),
        out_shape=jax.ShapeDtypeStruct((m, f), BF16),
        compiler_params=_params("parallel", "parallel"),
    )(*args)


def _mm_residual_kernel(*refs, scale, emit):
    if emit:
        x_ref, w_ref, r_ref, g_ref, o_ref, hg_ref, ssq_ref = refs
    else:
        x_ref, w_ref, r_ref, o_ref = refs
    h = r_ref[...] + scale * _dot(x_ref[...], w_ref[...])
    o_ref[...] = h
    if emit:
        hg_ref[...] = (h * g_ref[...]).astype(hg_ref.dtype)
        part = jnp.broadcast_to(jnp.sum(h * h, axis=1, keepdims=True), ssq_ref.shape)

        @pl.when(pl.program_id(1) == 0)
        def _():
            ssq_ref[...] = part

        @pl.when(pl.program_id(1) > 0)
        def _():
            ssq_ref[...] += part


def _mm_residual(x, w, layer, res, scale, next_gain=None, tm=1024, tn=512):
    m, k = x.shape
    n = w.shape[2]
    tm, tn = min(tm, m), min(tn, n)
    emit = next_gain is not None
    tile = pl.BlockSpec((tm, tn), lambda i, j: (i, j))
    in_specs = [pl.BlockSpec((tm, k), lambda i, j: (i, 0)), _weight_tile(w, layer, tn), tile]
    args = [x, w, res]
    out_specs, out_shape = tile, jax.ShapeDtypeStruct((m, n), F32)
    if emit:
        in_specs.append(pl.BlockSpec((1, tn), lambda i, j: (0, j)))
        args.append(next_gain.reshape(1, n))
        out_specs = [tile, tile, pl.BlockSpec((tm, LANES), lambda i, j: (i, 0))]
        out_shape = [out_shape, jax.ShapeDtypeStruct((m, n), BF16), jax.ShapeDtypeStruct((m, LANES), F32)]
    return pl.pallas_call(
        functools.partial(_mm_residual_kernel, scale=scale, emit=emit),
        grid=(m // tm, n // tn),
        in_specs=in_specs,
        out_specs=out_specs,
        out_shape=out_shape,
        compiler_params=_params("parallel", "arbitrary"),
    )(*args)


def _gain_ssq_kernel(x_ref, g_ref, hg_ref, ssq_ref):
    x = x_ref[...]
    hg_ref[...] = (x * g_ref[...]).astype(hg_ref.dtype)
    ssq_ref[...] = jnp.broadcast_to(jnp.sum(x * x, axis=1, keepdims=True), ssq_ref.shape)


def _gain_ssq(x, g, rows=256):
    t, d = x.shape
    rows = min(rows, t)
    return pl.pallas_call(
        _gain_ssq_kernel,
        grid=(t // rows,),
        in_specs=[pl.BlockSpec((rows, d), lambda i: (i, 0)), pl.BlockSpec((1, d), lambda i: (0, 0))],
        out_specs=[pl.BlockSpec((rows, d), lambda i: (i, 0)), pl.BlockSpec((rows, LANES), lambda i: (i, 0))],
        out_shape=[jax.ShapeDtypeStruct((t, d), BF16), jax.ShapeDtypeStruct((t, LANES), F32)],
        compiler_params=_params("parallel"),
    )(x, g.reshape(1, d))


def _rows(start, dilation):
    return pl.ds(start, BAND, stride=dilation) if dilation > 1 else pl.ds(start, BAND)


def _attn_kernel(slope_ref, *refs):
    ins = refs[:5 * ATT_GROUPS]
    y_ref, o_scr, l_scr = refs[5 * ATT_GROUPS:]
    first_segment = pl.program_id(1) == 0
    row = lax.broadcasted_iota(jnp.int32, (BAND, BAND), 0)
    col = lax.broadcasted_iota(jnp.int32, (BAND, BAND), 1)
    lag_c = row - col
    lag_p = lag_c + BAND
    valid_c = lag_c >= 0
    valid_p_any = lag_p <= BAND
    valid_p_first = lag_p <= jnp.where(first_segment, -1, BAND)
    scale = HEAD_DIM ** -0.5
    for g, (window, dilation) in enumerate(ATT_PATTERNS):
        q_ref, kp_ref, kc_ref, vp_ref, vc_ref = ins[5 * g:5 * g + 5]
        slope = slope_ref[0, :, g:g + 1] * float(dilation)
        bias_c = slope * lag_c.astype(F32)
        bias_p = slope * lag_p.astype(F32)
        span = BAND * dilation
        bias = jnp.concatenate([bias_p, bias_c], axis=1)
        valid_any = jnp.concatenate([valid_p_any, valid_c], axis=1)
        valid_first = jnp.concatenate([valid_p_first, valid_c], axis=1)
        for r in range(dilation):
            k_prev = kp_ref[0, _rows(r, dilation), :].astype(BF16)
            v_prev = vp_ref[0, _rows(r, dilation), :].astype(BF16)
            for blk in range(SEGMENT // span):
                rows = _rows(blk * span + r, dilation)
                q = q_ref[0, rows, :].astype(BF16)
                k_own = kc_ref[0, rows, :].astype(BF16)
                v_own = vc_ref[0, rows, :].astype(BF16)
                scores = _dot_nt(q, jnp.concatenate([k_prev, k_own], axis=0)) * scale - bias
                scores = jnp.where(valid_first if blk == 0 else valid_any, scores, -jnp.inf)
                m = jnp.max(scores, axis=1, keepdims=True)
                p = jnp.exp(scores - m)
                denom = jnp.sum(p, axis=1, keepdims=True)
                out = _dot(p.astype(BF16), jnp.concatenate([v_prev, v_own], axis=0))
                o_scr[g, rows, :] = out / denom
                l_scr[g, rows, :] = jnp.broadcast_to(m + jnp.log(denom), (BAND, HEAD_DIM))
                k_prev, v_prev = k_own, v_own
    l0, l1, l2 = l_scr[0], l_scr[1], l_scr[2]
    m = jnp.maximum(jnp.maximum(l0, l1), l2)
    e0, e1, e2 = jnp.exp(l0 - m), jnp.exp(l1 - m), jnp.exp(l2 - m)
    y = (e0 * o_scr[0] + e1 * o_scr[1] + e2 * o_scr[2]) / (e0 + e1 + e2)
    y_ref[0] = y.astype(y_ref.dtype)


def _dilated_attention(qkv, batch, seq):
    view = qkv.reshape(batch, seq, 3 * ATT_WIDTH)
    heads = ATT_WIDTH // HEAD_DIM
    slopes = jnp.asarray(
        [[[2.0 ** (-8.0 * (g * ATT_GROUP_HEADS + hh + 1) / ATT_HEADS) if g < ATT_GROUPS else 0.0
           for g in range(LANES)]] for hh in range(ATT_GROUP_HEADS)], F32)
    in_specs = [pl.BlockSpec((1, 1, LANES), lambda b, s, hh: (hh, 0, 0))]
    for g, (window, dilation) in enumerate(ATT_PATTERNS):
        assert window // dilation == BAND and SEGMENT % (BAND * dilation) == 0
        span = BAND * dilation
        per_seg = SEGMENT // span

        def cur(which, g=g):
            return pl.BlockSpec((1, SEGMENT, HEAD_DIM),
                                lambda b, s, hh: (b, s, which * heads + g * ATT_GROUP_HEADS + hh))

        def prev(which, g=g, span=span, per_seg=per_seg):
            return pl.BlockSpec((1, span, HEAD_DIM),
                                lambda b, s, hh: (b, jnp.maximum(s * per_seg - 1, 0),
                                                  which * heads + g * ATT_GROUP_HEADS + hh))

        in_specs += [cur(0), prev(1), cur(1), prev(2), cur(2)]
    out = pl.pallas_call(
        _attn_kernel,
        grid=(batch, seq // SEGMENT, ATT_GROUP_HEADS),
        in_specs=in_specs,
        out_specs=pl.BlockSpec((1, SEGMENT, HEAD_DIM), lambda b, s, hh: (b, s, hh)),
        out_shape=jax.ShapeDtypeStruct((batch, seq, ATT_GROUP_WIDTH), BF16),
        scratch_shapes=[pltpu.VMEM((ATT_GROUPS, SEGMENT, HEAD_DIM), F32),
                        pltpu.VMEM((ATT_GROUPS, SEGMENT, HEAD_DIM), F32)],
        compiler_params=_params("parallel", "parallel", "parallel"),
    )(slopes, *([view] * (5 * ATT_GROUPS)))
    return out.reshape(batch * seq, ATT_GROUP_WIDTH)


def _causal_conv(x, tail_ref, w_ref, b_ref):
    rows = x.shape[0]
    ext = jnp.concatenate([tail_ref[...], x], axis=0)
    y = w_ref[CONV_WIDTH - 1:CONV_WIDTH, :] * x + b_ref[...]
    for back in range(1, CONV_WIDTH):
        shifted = pltpu.roll(ext, back, axis=0)[SUBLANES:SUBLANES + rows]
        y = y + w_ref[CONV_WIDTH - 1 - back:CONV_WIDTH - back, :] * shifted
    tail_ref[...] = x[rows - SUBLANES:rows]
    return y


def _lru_kernel(x_ref, g_ref, cw_ref, cb_ref, wa_ref, ba_ref, wx_ref, bx_ref, lam_ref, o_ref,
                tail_ref, h_ref):
    @pl.when(pl.program_id(2) == 0)
    def _():
        tail_ref[...] = jnp.zeros_like(tail_ref)
        h_ref[...] = jnp.zeros_like(h_ref)

    rows, width = x_ref.shape[1], x_ref.shape[2]
    xc = _causal_conv(x_ref[0], tail_ref, cw_ref, cb_ref)
    xb = xc.astype(BF16)
    pre_a, pre_x = [], []
    for blk in range(width // LRU_BLOCK):
        sl = slice(blk * LRU_BLOCK, (blk + 1) * LRU_BLOCK)
        pre_a.append(_dot(xb[:, sl], wa_ref[blk]))
        pre_x.append(_dot(xb[:, sl], wx_ref[blk]))
    r = jax.nn.sigmoid(jnp.concatenate(pre_a, axis=1) + ba_ref[...])
    i = jax.nn.sigmoid(jnp.concatenate(pre_x, axis=1) + bx_ref[...])
    log_a = -LRU_C * r * _softplus(-lam_ref[...])
    a = jnp.exp(log_a)
    one_minus_a2 = 1.0 - a * a
    b = jnp.where(one_minus_a2 > 0.0, one_minus_a2 * lax.rsqrt(one_minus_a2), 0.0) * (i * xc)
    sub = lax.broadcasted_iota(jnp.int32, (SUBLANES, width), 0)
    steps = [(step, sub >= step) for step in (1, 2, 4)]
    carry = h_ref[...]
    hs = []
    for grp in range(rows // SUBLANES):
        sl = slice(grp * SUBLANES, (grp + 1) * SUBLANES)
        a_grp, b_grp = a[sl], b[sl]
        for step, keep in steps:
            a_prev = pltpu.roll(a_grp, step, axis=0)
            b_prev = pltpu.roll(b_grp, step, axis=0)
            b_grp = jnp.where(keep, a_grp * b_prev + b_grp, b_grp)
            a_grp = jnp.where(keep, a_grp * a_prev, a_grp)
        h_grp = a_grp * carry + b_grp
        hs.append(h_grp)
        carry = h_grp[SUBLANES - 1:SUBLANES]
    h_ref[...] = carry
    h = jnp.concatenate(hs, axis=0)
    o_ref[0] = (jax.nn.gelu(g_ref[0]) * h).astype(o_ref.dtype)


def _lru_branch(proj, batch, seq, x_col, g_col, width, p, layer, rows=256, cols=512):
    rows, cols = min(rows, seq), min(cols, width)
    view = proj.reshape(batch, seq, proj.shape[1])
    nblk = cols // LRU_BLOCK
    vec = lambda: pl.BlockSpec((1, cols), lambda b, c, t: (0, c))
    gate_w = lambda: pl.BlockSpec((None, nblk, LRU_BLOCK, LRU_BLOCK), lambda b, c, t: (layer, c, 0, 0))
    out = pl.pallas_call(
        _lru_kernel,
        grid=(batch, width // cols, seq // rows),
        in_specs=[pl.BlockSpec((1, rows, cols), lambda b, c, t: (b, t, x_col // cols + c)),
                  pl.BlockSpec((1, rows, cols), lambda b, c, t: (b, t, g_col // cols + c)),
                  pl.BlockSpec((CONV_WIDTH, cols), lambda b, c, t: (0, c)),
                  vec(), gate_w(), vec(), gate_w(), vec(), vec()],
        out_specs=pl.BlockSpec((1, rows, cols), lambda b, c, t: (b, t, c)),
        out_shape=jax.ShapeDtypeStruct((batch, seq, width), BF16),
        scratch_shapes=[pltpu.VMEM((SUBLANES, cols), F32), pltpu.VMEM((1, cols), F32)],
        compiler_params=_params("parallel", "parallel", "arbitrary"),
    )(view, view, p["conv_w"], p["conv_b"].reshape(1, width), p["w_a"], p["b_a"].reshape(1, width),
      p["w_x"], p["b_x"].reshape(1, width), p["lam"].reshape(1, width))
    return out.reshape(batch * seq, width)


def _ssd_kernel(x_ref, bc_ref, z_ref, dt_ref, xw_ref, xb_ref, bcw_ref, bcb_ref, dtb_ref, alog_ref, dskip_ref,
                ng_ref, o_ref, xtail_ref, bctail_ref, st_ref):
    @pl.when(pl.program_id(1) == 0)
    def _():
        xtail_ref[...] = jnp.zeros_like(xtail_ref)
        bctail_ref[...] = jnp.zeros_like(bctail_ref)
        st_ref[...] = jnp.zeros_like(st_ref)

    chunk = SSD_CHUNK
    gw = SSD_GROUPS * SSD_STATE
    d_inner = x_ref.shape[2]
    group_width = d_inner // SSD_GROUPS
    pairs = group_width // LANES
    xs = _silu(_causal_conv(x_ref[0], xtail_ref, xw_ref, xb_ref))
    bc = _silu(_causal_conv(bc_ref[0], bctail_ref, bcw_ref, bcb_ref))
    bm = bc[:, :gw]
    cm = bc[:, gw:]

    dt = _softplus(dt_ref[0] + dtb_ref[...])
    da = dt * (-jnp.exp(alog_ref[...]))
    row = lax.broadcasted_iota(jnp.int32, (chunk, LANES), 0)
    col = lax.broadcasted_iota(jnp.int32, (chunk, LANES), 1)
    cs = da
    step = 1
    while step < chunk:
        cs = cs + jnp.where(row >= step, pltpu.roll(cs, step, axis=0), 0.0)
        step *= 2
    cs_t = cs.T
    dt_t = dt.T
    to_end_t = jnp.exp(cs_t[:, chunk - 1:chunk] - cs_t) * dt_t
    from_start = jnp.exp(cs)
    causal = row >= col
    left = col < SSD_HEAD_DIM

    ys = []
    for g in range(SSD_GROUPS):
        gs = slice(g * SSD_STATE, (g + 1) * SSD_STATE)
        b_g, c_g = bm[:, gs], cm[:, gs]
        cb = _dot_nt(c_g.astype(BF16), b_g.astype(BF16))
        b_t = b_g.T
        for pr in range(pairs):
            ps = slice(g * group_width + pr * LANES, g * group_width + (pr + 1) * LANES)
            x_pair = xs[:, ps]
            st_pair = st_ref[g, :, pr * LANES:(pr + 1) * LANES]
            rhs = jnp.concatenate([x_pair, st_pair], axis=0).astype(BF16)
            x_pair_b = x_pair.astype(BF16)
            y_half, st_half = [], []
            for half in range(2):
                h = (g * pairs + pr) * 2 + half
                decay = jnp.where(causal, jnp.exp(cs[:, h:h + 1] - cs_t[h:h + 1, :]), 0.0)
                m_h = decay * cb * dt_t[h:h + 1, :]
                c_h = c_g * from_start[:, h:h + 1]
                lhs = jnp.concatenate([m_h, c_h], axis=1).astype(BF16)
                y_half.append(_dot(lhs, rhs))
                local = _dot((b_t * to_end_t[h:h + 1, :]).astype(BF16), x_pair_b)
                st_half.append(from_start[chunk - 1:chunk, h:h + 1] * st_pair + local)
            ys.append(jnp.where(left, y_half[0], y_half[1]))
            st_ref[g, :, pr * LANES:(pr + 1) * LANES] = jnp.where(left, st_half[0], st_half[1])
    y = jnp.concatenate(ys, axis=1)
    y = y + xs * dskip_ref[...]
    y = y * _silu(z_ref[0])
    outs = []
    for g in range(SSD_GROUPS):
        yg = y[:, g * group_width:(g + 1) * group_width]
        ms = jnp.mean(yg * yg, axis=-1, keepdims=True)
        outs.append(yg * lax.rsqrt(ms + NORM_EPS))
    o_ref[0] = (jnp.concatenate(outs, axis=1) * ng_ref[...]).astype(o_ref.dtype)


def _ssd_branch(proj, dt_proj, dt_col, batch, seq, z_col, xbc_col, d_inner, p):
    bc_width = 2 * SSD_GROUPS * SSD_STATE
    bc_col = xbc_col + d_inner
    heads = d_inner // SSD_HEAD_DIM
    view = proj.reshape(batch, seq, proj.shape[1])
    dt_view = dt_proj.reshape(batch, seq, dt_proj.shape[1])
    pad = lambda v: jnp.pad(v, (0, LANES - heads)).reshape(1, LANES)
    const = lambda shape: pl.BlockSpec(shape, lambda b, c: (0, 0))
    conv_w, conv_b = p["conv_w"], p["conv_b"].reshape(1, d_inner + bc_width)
    out = pl.pallas_call(
        _ssd_kernel,
        grid=(batch, seq // SSD_CHUNK),
        in_specs=[pl.BlockSpec((1, SSD_CHUNK, d_inner), lambda b, c: (b, c, xbc_col // d_inner)),
                  pl.BlockSpec((1, SSD_CHUNK, bc_width), lambda b, c: (b, c, bc_col // bc_width)),
                  pl.BlockSpec((1, SSD_CHUNK, d_inner), lambda b, c: (b, c, z_col // d_inner)),
                  pl.BlockSpec((1, SSD_CHUNK, LANES), lambda b, c: (b, c, dt_col // LANES)),
                  const((CONV_WIDTH, d_inner)), const((1, d_inner)),
                  const((CONV_WIDTH, bc_width)), const((1, bc_width)),
                  const((1, LANES)), const((1, LANES)), const((1, d_inner)), const((1, d_inner))],
        out_specs=pl.BlockSpec((1, SSD_CHUNK, d_inner), lambda b, c: (b, c, 0)),
        out_shape=jax.ShapeDtypeStruct((batch, seq, d_inner), BF16),
        scratch_shapes=[pltpu.VMEM((SUBLANES, d_inner), F32), pltpu.VMEM((SUBLANES, bc_width), F32),
                        pltpu.VMEM((SSD_GROUPS, SSD_STATE, d_inner // SSD_GROUPS), F32)],
        compiler_params=_params("parallel", "arbitrary"),
    )(view, view, view, dt_view, conv_w[:, :d_inner], conv_b[:, :d_inner], conv_w[:, d_inner:],
      conv_b[:, d_inner:], pad(p["dt_bias"]), pad(p["a_log"]),
      jnp.repeat(p["d_skip"], SSD_HEAD_DIM).reshape(1, d_inner), p["norm"].reshape(1, d_inner))
    return out.reshape(batch * seq, d_inner)


def _merge_kernel(gl_ref, ya_ref, yl_ref, ys_ref, gu0_ref, gu1_ref, gu2_ref, gb0_ref, gb1_ref, gb2_ref,
                  pa_ref, pl_ref, ps_ref, o_ref):
    g_low = gl_ref[...].astype(BF16)

    def gated(gu_ref, gb_ref, y_ref, p_ref):
        return jax.nn.sigmoid(_dot(g_low, gu_ref[...]) + gb_ref[...]) * _dot(y_ref[...], p_ref[...])

    y = (gated(gu0_ref, gb0_ref, ya_ref, pa_ref) + gated(gu1_ref, gb1_ref, yl_ref, pl_ref)
         + gated(gu2_ref, gb2_ref, ys_ref, ps_ref))
    o_ref[...] = y.astype(o_ref.dtype)


def _gated_merge(g_low, y_attn, y_lru, y_ssd, gate_up, gate_b, p_attn, p_lru, p_ssd, layer, tm=1024, tn=512):
    m = y_attn.shape[0]
    d = p_attn.shape[2]
    tm, tn = min(tm, m), min(tn, d)
    nd = d // tn
    rows = lambda width: pl.BlockSpec((tm, width), lambda i, j: (i, 0))
    gate_w = lambda br: _weight_tile(gate_up, layer, tn, lambda j: br * nd + j)
    gate_bias = lambda br: pl.BlockSpec((1, tn), lambda i, j: (0, br * nd + j))
    gb = gate_b.reshape(1, 3 * d)
    return pl.pallas_call(
        _merge_kernel,
        grid=(m // tm, nd),
        in_specs=[rows(GATE_RANK), rows(y_attn.shape[1]), rows(y_lru.shape[1]),
                  rows(y_ssd.shape[1]),
                  gate_w(0), gate_w(1), gate_w(2), gate_bias(0), gate_bias(1), gate_bias(2),
                  _weight_tile(p_attn, layer, tn), _weight_tile(p_lru, layer, tn),
                  _weight_tile(p_ssd, layer, tn)],
        out_specs=pl.BlockSpec((tm, tn), lambda i, j: (i, j)),
        out_shape=jax.ShapeDtypeStruct((m, d), BF16),
        compiler_params=_params("parallel", "parallel"),
    )(g_low, y_attn, y_lru, y_ssd, gate_up, gate_up, gate_up, gb, gb, gb, p_attn, p_lru, p_ssd)


def _rms_normalize(x, g):
    ms = jnp.mean(x * x, axis=-1, keepdims=True)
    return x * lax.rsqrt(ms + NORM_EPS) * g


def _xattn_kernel(h_ref, g_ref, wq_ref, kv_ref, wo_ref, gn_ref, o_ref, on_ref):
    h = h_ref[0]
    hn = _rms_normalize(h, g_ref[...]).astype(BF16)
    q = _dot(hn, wq_ref[...]).astype(BF16)
    width = XATTN_HEADS * XATTN_HEAD_DIM
    scale = XATTN_HEAD_DIM ** -0.5
    outs = []
    for hh in range(XATTN_HEADS):
        sl = slice(hh * XATTN_HEAD_DIM, (hh + 1) * XATTN_HEAD_DIM)
        vsl = slice(width + hh * XATTN_HEAD_DIM, width + (hh + 1) * XATTN_HEAD_DIM)
        s = _dot_nt(q[:, sl], kv_ref[0, :, sl]) * scale
        p = jnp.exp(s - jnp.max(s, axis=1, keepdims=True))
        denom = jnp.sum(p, axis=1, keepdims=True)
        outs.append(_dot(p.astype(BF16), kv_ref[0, :, vsl]) / denom)
    o = jnp.concatenate(outs, axis=1).astype(BF16)
    h_new = h + _dot(o, wo_ref[...])
    o_ref[0] = h_new
    on_ref[0] = _rms_normalize(h_new, gn_ref[...]).astype(on_ref.dtype)


def _memory_cross_attention(h, batch, seq, norm_g, w_q, kv, w_o, layer, next_gain, rows=256):
    d = h.shape[1]
    rows = min(rows, seq)
    width = XATTN_HEADS * XATTN_HEAD_DIM
    mem_len = kv.shape[0] // batch
    tile = pl.BlockSpec((1, rows, d), lambda b, i: (b, i, 0))
    gain = pl.BlockSpec((1, d), lambda b, i: (0, 0))
    out, normed = pl.pallas_call(
        _xattn_kernel,
        grid=(batch, seq // rows),
        in_specs=[tile, gain,
                  pl.BlockSpec((None, d, width), lambda b, i: (layer, 0, 0)),
                  pl.BlockSpec((1, mem_len, 2 * width), lambda b, i: (b, 0, 0)),
                  pl.BlockSpec((None, width, d), lambda b, i: (layer, 0, 0)),
                  gain],
        out_specs=[tile, tile],
        out_shape=[jax.ShapeDtypeStruct((batch, seq, d), F32), jax.ShapeDtypeStruct((batch, seq, d), BF16)],
        compiler_params=_params("parallel", "parallel"),
    )(h.reshape(batch, seq, d), norm_g.reshape(1, d), w_q, kv.reshape(batch, mem_len, 2 * width), w_o,
      next_gain.reshape(1, d))
    return out.reshape(batch * seq, d), normed.reshape(batch * seq, d)


def _layer(h, hg, ssq, mem, w, p, layer, next_gain, batch, seq):
    lru_width = p["lru_conv_w"].shape[1]
    d_inner = p["ssd_norm"].shape[0]

    hidden = _mm_swiglu(hg, w["ffn1_w_gu"], layer, ssq)
    h, ug, ussq = _mm_residual(hidden, w["ffn1_w_down"], layer, h, 0.5, p["mix_norm"])

    qkv = _mm(ug, w["in_qkv"], layer, F32, ussq, tn=768)
    proj = _mm(ug, w["in_branches"], layer, F32, ussq)
    small = _mm(ug, w["in_small"], layer, F32, ussq)

    y_attn = _dilated_attention(qkv, batch, seq)
    y_lru = _lru_branch(proj, batch, seq, 0, lru_width, lru_width,
                        dict(conv_w=p["lru_conv_w"], conv_b=p["lru_conv_b"], w_a=w["lru_w_a"], b_a=p["lru_b_a"],
                             w_x=w["lru_w_x"], b_x=p["lru_b_x"], lam=p["lru_lambda"]), layer)
    y_ssd = _ssd_branch(proj, small, GATE_RANK, batch, seq, 2 * lru_width, 2 * lru_width + d_inner, d_inner,
                        dict(conv_w=p["ssd_conv_w"], conv_b=p["ssd_conv_b"], dt_bias=p["ssd_dt_bias"],
                             a_log=p["ssd_a_log"], d_skip=p["ssd_d"], norm=p["ssd_norm"]))
    merged = _gated_merge(small, y_attn, y_lru, y_ssd, w["gate_up"], p["gate_b"],
                          w["w_br_attn"], w["w_br_lru"], w["w_br_ssd"], layer)
    h = _mm_residual(merged, w["w_out"], layer, h, 1.0)

    memn = _rmsnorm(mem, p["mem_norm"], BF16)
    kv = _mm(memn, w["xattn_w_kv"], layer, BF16)
    h, hn = _memory_cross_attention(h, batch, seq, p["xattn_norm"], w["xattn_w_q"], kv, w["xattn_w_o"], layer,
                                    p["ffn2_norm"])

    hidden = _mm_swiglu(hn, w["ffn2_w_gu"], layer)
    if next_gain is None:
        return _mm_residual(hidden, w["ffn2_w_down"], layer, h, 0.5), None, None
    return _mm_residual(hidden, w["ffn2_w_down"], layer, h, 0.5, next_gain)


def kernel(x, mem, ffn1_norm, ffn1_w_gu, ffn1_w_down, mix_norm, w_in, lru_conv_w, lru_conv_b, lru_w_a, lru_b_a, lru_w_x, lru_b_x, lru_lambda, ssd_conv_w, ssd_conv_b, ssd_dt_bias, ssd_a_log, ssd_d, ssd_norm, gate_up, gate_b, w_br_attn, w_br_lru, w_br_ssd, w_out, xattn_norm, mem_norm, xattn_w_q, xattn_w_kv, xattn_w_o, ffn2_norm, ffn2_w_gu, ffn2_w_down, final_norm):
    batch, seq, d = x.shape
    depth = w_in.shape[0]
    heads = ssd_dt_bias.shape[1]
    c_lru = 3 * ATT_WIDTH
    c_dt = c_lru + 2 * lru_conv_w.shape[2] + ssd_norm.shape[1] + ssd_conv_w.shape[2]
    c_gate = c_dt + heads
    in_small = jnp.concatenate([w_in[:, :, c_gate:], w_in[:, :, c_dt:c_gate]], axis=2)
    in_small = jnp.pad(in_small, ((0, 0), (0, 0), (0, LANES - heads)))
    weights = dict(
        ffn1_w_gu=ffn1_w_gu, ffn1_w_down=ffn1_w_down, in_qkv=w_in[:, :, :c_lru],
        in_branches=w_in[:, :, c_lru:c_dt], in_small=in_small, lru_w_a=lru_w_a, lru_w_x=lru_w_x,
        gate_up=gate_up, w_br_attn=w_br_attn, w_br_lru=w_br_lru, w_br_ssd=w_br_ssd, w_out=w_out,
        xattn_w_q=xattn_w_q, xattn_w_kv=xattn_w_kv, xattn_w_o=xattn_w_o, ffn2_w_gu=ffn2_w_gu,
        ffn2_w_down=ffn2_w_down)
    weights = {name: value.astype(BF16) for name, value in weights.items()}
    small = dict(
        mix_norm=mix_norm, lru_conv_w=lru_conv_w, lru_conv_b=lru_conv_b, lru_b_a=lru_b_a, lru_b_x=lru_b_x,
        lru_lambda=lru_lambda, ssd_conv_w=ssd_conv_w, ssd_conv_b=ssd_conv_b, ssd_dt_bias=ssd_dt_bias,
        ssd_a_log=ssd_a_log, ssd_d=ssd_d, ssd_norm=ssd_norm, gate_b=gate_b, xattn_norm=xattn_norm,
        mem_norm=mem_norm, ffn2_norm=ffn2_norm)
    mem2 = mem.reshape(batch * mem.shape[1], d)
    h = x.reshape(batch * seq, d)
    hg, ssq = _gain_ssq(h, ffn1_norm[0])
    for layer in range(depth):
        p = {name: value[layer] for name, value in small.items()}
        next_gain = ffn1_norm[layer + 1] if layer + 1 < depth else None
        h, hg, ssq = _layer(h, hg, ssq, mem2, weights, p, layer, next_gain, batch, seq)
    return _rmsnorm(h, final_norm, F32).reshape(batch, seq, d)
```

```python
import functools

import jax
import jax.numpy as jnp
from jax import lax
from jax.experimental import pallas as pl
from jax.experimental.pallas import tpu as pltpu

F32 = jnp.float32
BF16 = jnp.bfloat16

HEAD_DIM = 128
ATT_PATTERNS = ((128, 1), (512, 4), (2048, 16))
ATT_GROUPS = 3
ATT_GROUP_HEADS = 4
ATT_HEADS = ATT_GROUPS * ATT_GROUP_HEADS
ATT_GROUP_WIDTH = ATT_GROUP_HEADS * HEAD_DIM
ATT_WIDTH = ATT_HEADS * HEAD_DIM
BAND = 128
SEGMENT = 2048
LRU_BLOCK = 128
LRU_C = 8.0
CONV_WIDTH = 4
SSD_HEAD_DIM = 64
SSD_GROUPS = 4
SSD_STATE = 128
SSD_CHUNK = 128
GATE_RANK = 512
XATTN_HEADS = 4
XATTN_HEAD_DIM = 128
NORM_EPS = 1e-6

LANES = 128
SUBLANES = 8
VMEM_LIMIT_BYTES = 56 * 1024 * 1024


def _params(*semantics):
    return pltpu.CompilerParams(dimension_semantics=semantics, vmem_limit_bytes=VMEM_LIMIT_BYTES)


def _softplus(x):
    return jnp.maximum(x, 0.0) + jnp.log1p(jnp.exp(-jnp.abs(x)))


def _silu(x):
    return x * jax.nn.sigmoid(x)


def _dot(a, b):
    return jnp.dot(a, b, preferred_element_type=F32)


def _dot_nt(a, b):
    return lax.dot_general(a, b, (((1,), (1,)), ((), ())), preferred_element_type=F32)


def _rmsnorm_kernel(x_ref, g_ref, o_ref):
    x = x_ref[...]
    ms = jnp.mean(x * x, axis=-1, keepdims=True)
    o_ref[...] = (x * lax.rsqrt(ms + NORM_EPS) * g_ref[...]).astype(o_ref.dtype)


def _rmsnorm(x, g, out_dtype, rows=256):
    t, d = x.shape
    rows = min(rows, t)
    return pl.pallas_call(
        _rmsnorm_kernel,
        grid=(t // rows,),
        in_specs=[pl.BlockSpec((rows, d), lambda i: (i, 0)),
                  pl.BlockSpec((1, d), lambda i: (0, 0))],
        out_specs=pl.BlockSpec((rows, d), lambda i: (i, 0)),
        out_shape=jax.ShapeDtypeStruct((t, d), out_dtype),
        compiler_params=_params("parallel"),
    )(x, g.reshape(1, d))


def _row_rsqrt(ssq_ref, inv_d):
    return lax.rsqrt(ssq_ref[:, :1] * inv_d + NORM_EPS)


def _mm_kernel(*refs, inv_d):
    if inv_d is None:
        x_ref, w_ref, o_ref = refs
        o_ref[...] = _dot(x_ref[...], w_ref[...]).astype(o_ref.dtype)
    else:
        x_ref, w_ref, ssq_ref, o_ref = refs
        o_ref[...] = (_dot(x_ref[...], w_ref[...]) * _row_rsqrt(ssq_ref, inv_d)).astype(o_ref.dtype)


def _weight_tile(stack, layer, tn, col=lambda j: j):
    return pl.BlockSpec((None, stack.shape[1], tn), lambda i, j: (layer, 0, col(j)))


def _mm(x, w, layer, out_dtype, ssq=None, tm=1024, tn=1024):
    m, k = x.shape
    n = w.shape[2]
    tm, tn = min(tm, m), min(tn, n)
    in_specs = [pl.BlockSpec((tm, k), lambda i, j: (i, 0)), _weight_tile(w, layer, tn)]
    args = [x, w]
    if ssq is not None:
        in_specs.append(pl.BlockSpec((tm, LANES), lambda i, j: (i, 0)))
        args.append(ssq)
    return pl.pallas_call(
        functools.partial(_mm_kernel, inv_d=None if ssq is None else 1.0 / k),
        grid=(m // tm, n // tn),
        in_specs=in_specs,
        out_specs=pl.BlockSpec((tm, tn), lambda i, j: (i, j)),
        out_shape=jax.ShapeDtypeStruct((m, n), out_dtype),
        compiler_params=_params("parallel", "parallel"),
    )(*args)


def _mm_swiglu_kernel(*refs, inv_d):
    if inv_d is None:
        x_ref, wg_ref, wu_ref, o_ref = refs
        scale = None
    else:
        x_ref, wg_ref, wu_ref, ssq_ref, o_ref = refs
        scale = _row_rsqrt(ssq_ref, inv_d)
    x = x_ref[...]
    gate = _dot(x, wg_ref[...])
    up = _dot(x, wu_ref[...])
    if scale is not None:
        gate, up = gate * scale, up * scale
    o_ref[...] = (_silu(gate) * up).astype(o_ref.dtype)


def _mm_swiglu(x, w_gu, layer, ssq=None, tm=1024, tn=512):
    m, k = x.shape
    f = w_gu.shape[2] // 2
    tm, tn = min(tm, m), min(tn, f)
    nf = f // tn
    in_specs = [pl.BlockSpec((tm, k), lambda i, j: (i, 0)),
                _weight_tile(w_gu, layer, tn),
                _weight_tile(w_gu, layer, tn, lambda j: j + nf)]
    args = [x, w_gu, w_gu]
    if ssq is not None:
        in_specs.append(pl.BlockSpec((tm, LANES), lambda i, j: (i, 0)))
        args.append(ssq)
    return pl.pallas_call(
        functools.partial(_mm_swiglu_kernel, inv_d=None if ssq is None else 1.0 / k),
        grid=(m // tm, nf),
        in_specs=in_specs,
        out_specs=pl.BlockSpec((tm, tn), lambda i, j: (i, j)),
        out_shape=jax.ShapeDtypeStruct((m, f), BF16),
        compiler_params=_params("parallel", "parallel"),
    )(*args)


def _mm_residual_kernel(*refs, scale, emit):
    if emit:
        x_ref, w_ref, r_ref, g_ref, o_ref, hg_ref, ssq_ref = refs
    else:
        x_ref, w_ref, r_ref, o_ref = refs
    h = r_ref[...] + scale * _dot(x_ref[...], w_ref[...])
    o_ref[...] = h
    if emit:
        hg_ref[...] = (h * g_ref[...]).astype(hg_ref.dtype)
        part = jnp.broadcast_to(jnp.sum(h * h, axis=1, keepdims=True), ssq_ref.shape)

        @pl.when(pl.program_id(1) == 0)
        def _():
            ssq_ref[...] = part

        @pl.when(pl.program_id(1) > 0)
        def _():
            ssq_ref[...] += part


def _mm_residual(x, w, layer, res, scale, next_gain=None, tm=1024, tn=512):
    m, k = x.shape
    n = w.shape[2]
    tm, tn = min(tm, m), min(tn, n)
    emit = next_gain is not None
    tile = pl.BlockSpec((tm, tn), lambda i, j: (i, j))
    in_specs = [pl.BlockSpec((tm, k), lambda i, j: (i, 0)), _weight_tile(w, layer, tn), tile]
    args = [x, w, res]
    out_specs, out_shape = tile, jax.ShapeDtypeStruct((m, n), F32)
    if emit:
        in_specs.append(pl.BlockSpec((1, tn), lambda i, j: (0, j)))
        args.append(next_gain.reshape(1, n))
        out_specs = [tile, tile, pl.BlockSpec((tm, LANES), lambda i, j: (i, 0))]
        out_shape = [out_shape, jax.ShapeDtypeStruct((m, n), BF16), jax.ShapeDtypeStruct((m, LANES), F32)]
    return pl.pallas_call(
        functools.partial(_mm_residual_kernel, scale=scale, emit=emit),
        grid=(m // tm, n // tn),
        in_specs=in_specs,
        out_specs=out_specs,
        out_shape=out_shape,
        compiler_params=_params("parallel", "arbitrary"),
    )(*args)


def _gain_ssq_kernel(x_ref, g_ref, hg_ref, ssq_ref):
    x = x_ref[...]
    hg_ref[...] = (x * g_ref[...]).astype(hg_ref.dtype)
    ssq_ref[...] = jnp.broadcast_to(jnp.sum(x * x, axis=1, keepdims=True), ssq_ref.shape)


def _gain_ssq(x, g, rows=256):
    t, d = x.shape
    rows = min(rows, t)
    return pl.pallas_call(
        _gain_ssq_kernel,
        grid=(t // rows,),
        in_specs=[pl.BlockSpec((rows, d), lambda i: (i, 0)), pl.BlockSpec((1, d), lambda i: (0, 0))],
        out_specs=[pl.BlockSpec((rows, d), lambda i: (i, 0)), pl.BlockSpec((rows, LANES), lambda i: (i, 0))],
        out_shape=[jax.ShapeDtypeStruct((t, d), BF16), jax.ShapeDtypeStruct((t, LANES), F32)],
        compiler_params=_params("parallel"),
    )(x, g.reshape(1, d))


def _rows(start, dilation):
    return pl.ds(start, BAND, stride=dilation) if dilation > 1 else pl.ds(start, BAND)


def _attn_kernel(slope_ref, *refs):
    ins = refs[:5 * ATT_GROUPS]
    y_ref, o_scr, l_scr = refs[5 * ATT_GROUPS:]
    first_segment = pl.program_id(1) == 0
    row = lax.broadcasted_iota(jnp.int32, (BAND, BAND), 0)
    col = lax.broadcasted_iota(jnp.int32, (BAND, BAND), 1)
    lag_c = row - col
    lag_p = lag_c + BAND
    valid_c = lag_c >= 0
    valid_p_any = lag_p <= BAND
    valid_p_first = lag_p <= jnp.where(first_segment, -1, BAND)
    scale = HEAD_DIM ** -0.5
    for g, (window, dilation) in enumerate(ATT_PATTERNS):
        q_ref, kp_ref, kc_ref, vp_ref, vc_ref = ins[5 * g:5 * g + 5]
        slope = slope_ref[0, :, g:g + 1] * float(dilation)
        bias_c = slope * lag_c.astype(F32)
        bias_p = slope * lag_p.astype(F32)
        span = BAND * dilation
        alibi_c = jnp.where(valid_c, -bias_c, -jnp.inf)
        mask_any = jnp.concatenate([jnp.where(valid_p_any, -bias_p, -jnp.inf), alibi_c], axis=1)
        mask_first = jnp.concatenate([jnp.where(valid_p_first, -bias_p, -jnp.inf), alibi_c], axis=1)
        for r in range(dilation):
            k_prev = kp_ref[0, _rows(r, dilation), :].astype(BF16)
            v_prev = vp_ref[0, _rows(r, dilation), :].astype(BF16)
            for blk in range(SEGMENT // span):
                rows = _rows(blk * span + r, dilation)
                q = q_ref[0, rows, :].astype(BF16)
                k_own = kc_ref[0, rows, :].astype(BF16)
                v_own = vc_ref[0, rows, :].astype(BF16)
                scores = (_dot_nt(q, jnp.concatenate([k_prev, k_own], axis=0)) * scale
                          + (mask_first if blk == 0 else mask_any))
                m = jnp.max(scores, axis=1, keepdims=True)
                p = jnp.exp(scores - m)
                denom = jnp.sum(p, axis=1, keepdims=True)
                out = _dot(p.astype(BF16), jnp.concatenate([v_prev, v_own], axis=0))
                o_scr[g, rows, :] = out / denom
                l_scr[g, rows, :] = jnp.broadcast_to(m + jnp.log(denom), (BAND, HEAD_DIM))
                k_prev, v_prev = k_own, v_own
    l0, l1, l2 = l_scr[0], l_scr[1], l_scr[2]
    m = jnp.maximum(jnp.maximum(l0, l1), l2)
    e0, e1, e2 = jnp.exp(l0 - m), jnp.exp(l1 - m), jnp.exp(l2 - m)
    y = (e0 * o_scr[0] + e1 * o_scr[1] + e2 * o_scr[2]) / (e0 + e1 + e2)
    y_ref[0] = y.astype(y_ref.dtype)


def _dilated_attention(qkv, batch, seq):
    view = qkv.reshape(batch, seq, 3 * ATT_WIDTH)
    heads = ATT_WIDTH // HEAD_DIM
    slopes = jnp.asarray(
        [[[2.0 ** (-8.0 * (g * ATT_GROUP_HEADS + hh + 1) / ATT_HEADS) if g < ATT_GROUPS else 0.0
           for g in range(LANES)]] for hh in range(ATT_GROUP_HEADS)], F32)
    in_specs = [pl.BlockSpec((1, 1, LANES), lambda b, s, hh: (hh, 0, 0))]
    for g, (window, dilation) in enumerate(ATT_PATTERNS):
        assert window // dilation == BAND and SEGMENT % (BAND * dilation) == 0
        span = BAND * dilation
        per_seg = SEGMENT // span

        def cur(which, g=g):
            return pl.BlockSpec((1, SEGMENT, HEAD_DIM),
                                lambda b, s, hh: (b, s, which * heads + g * ATT_GROUP_HEADS + hh))

        def prev(which, g=g, span=span, per_seg=per_seg):
            return pl.BlockSpec((1, span, HEAD_DIM),
                                lambda b, s, hh: (b, jnp.maximum(s * per_seg - 1, 0),
                                                  which * heads + g * ATT_GROUP_HEADS + hh))

        in_specs += [cur(0), prev(1), cur(1), prev(2), cur(2)]
    out = pl.pallas_call(
        _attn_kernel,
        grid=(batch, seq // SEGMENT, ATT_GROUP_HEADS),
        in_specs=in_specs,
        out_specs=pl.BlockSpec((1, SEGMENT, HEAD_DIM), lambda b, s, hh: (b, s, hh)),
        out_shape=jax.ShapeDtypeStruct((batch, seq, ATT_GROUP_WIDTH), BF16),
        scratch_shapes=[pltpu.VMEM((ATT_GROUPS, SEGMENT, HEAD_DIM), F32),
                        pltpu.VMEM((ATT_GROUPS, SEGMENT, HEAD_DIM), F32)],
        compiler_params=_params("parallel", "parallel", "parallel"),
    )(slopes, *([view] * (5 * ATT_GROUPS)))
    return out.reshape(batch * seq, ATT_GROUP_WIDTH)


def _causal_conv(x, tail_ref, w_ref, b_ref):
    rows = x.shape[0]
    ext = jnp.concatenate([tail_ref[...], x], axis=0)
    y = w_ref[CONV_WIDTH - 1:CONV_WIDTH, :] * x + b_ref[...]
    for back in range(1, CONV_WIDTH):
        shifted = pltpu.roll(ext, back, axis=0)[SUBLANES:SUBLANES + rows]
        y = y + w_ref[CONV_WIDTH - 1 - back:CONV_WIDTH - back, :] * shifted
    tail_ref[...] = x[rows - SUBLANES:rows]
    return y


def _lru_kernel(x_ref, g_ref, cw_ref, cb_ref, wa_ref, ba_ref, wx_ref, bx_ref, lam_ref, o_ref,
                tail_ref, h_ref):
    @pl.when(pl.program_id(2) == 0)
    def _():
        tail_ref[...] = jnp.zeros_like(tail_ref)
        h_ref[...] = jnp.zeros_like(h_ref)

    rows, width = x_ref.shape[1], x_ref.shape[2]
    xc = _causal_conv(x_ref[0], tail_ref, cw_ref, cb_ref)
    xb = xc.astype(BF16)
    pre_a, pre_x = [], []
    for blk in range(width // LRU_BLOCK):
        sl = slice(blk * LRU_BLOCK, (blk + 1) * LRU_BLOCK)
        pre_a.append(_dot(xb[:, sl], wa_ref[blk]))
        pre_x.append(_dot(xb[:, sl], wx_ref[blk]))
    r = jax.nn.sigmoid(jnp.concatenate(pre_a, axis=1) + ba_ref[...])
    i = jax.nn.sigmoid(jnp.concatenate(pre_x, axis=1) + bx_ref[...])
    log_a = -LRU_C * r * _softplus(-lam_ref[...])
    a = jnp.exp(log_a)
    one_minus_a2 = 1.0 - a * a
    b = jnp.where(one_minus_a2 > 0.0, one_minus_a2 * lax.rsqrt(one_minus_a2), 0.0) * (i * xc)
    sub = lax.broadcasted_iota(jnp.int32, (SUBLANES, width), 0)
    steps = [(step, sub >= step) for step in (1, 2, 4)]
    carry = h_ref[...]
    hs = []
    for grp in range(rows // SUBLANES):
        sl = slice(grp * SUBLANES, (grp + 1) * SUBLANES)
        a_grp, b_grp = a[sl], b[sl]
        for step, keep in steps:
            a_prev = pltpu.roll(a_grp, step, axis=0)
            b_prev = pltpu.roll(b_grp, step, axis=0)
            b_grp = jnp.where(keep, a_grp * b_prev + b_grp, b_grp)
            a_grp = jnp.where(keep, a_grp * a_prev, a_grp)
        h_grp = a_grp * carry + b_grp
        hs.append(h_grp)
        carry = h_grp[SUBLANES - 1:SUBLANES]
    h_ref[...] = carry
    h = jnp.concatenate(hs, axis=0)
    o_ref[0] = (jax.nn.gelu(g_ref[0]) * h).astype(o_ref.dtype)


def _lru_branch(proj, batch, seq, x_col, g_col, width, p, layer, rows=512, cols=512):
    rows, cols = min(rows, seq), min(cols, width)
    view = proj.reshape(batch, seq, proj.shape[1])
    nblk = cols // LRU_BLOCK
    vec = lambda: pl.BlockSpec((1, cols), lambda b, c, t: (0, c))
    gate_w = lambda: pl.BlockSpec((None, nblk, LRU_BLOCK, LRU_BLOCK), lambda b, c, t: (layer, c, 0, 0))
    out = pl.pallas_call(
        _lru_kernel,
        grid=(batch, width // cols, seq // rows),
        in_specs=[pl.BlockSpec((1, rows, cols), lambda b, c, t: (b, t, x_col // cols + c)),
                  pl.BlockSpec((1, rows, cols), lambda b, c, t: (b, t, g_col // cols + c)),
                  pl.BlockSpec((CONV_WIDTH, cols), lambda b, c, t: (0, c)),
                  vec(), gate_w(), vec(), gate_w(), vec(), vec()],
        out_specs=pl.BlockSpec((1, rows, cols), lambda b, c, t: (b, t, c)),
        out_shape=jax.ShapeDtypeStruct((batch, seq, width), BF16),
        scratch_shapes=[pltpu.VMEM((SUBLANES, cols), F32), pltpu.VMEM((1, cols), F32)],
        compiler_params=_params("parallel", "parallel", "arbitrary"),
    )(view, view, p["conv_w"], p["conv_b"].reshape(1, width), p["w_a"], p["b_a"].reshape(1, width),
      p["w_x"], p["b_x"].reshape(1, width), p["lam"].reshape(1, width))
    return out.reshape(batch * seq, width)


def _ssd_kernel(x_ref, bc_ref, z_ref, dt_ref, xw_ref, xb_ref, bcw_ref, bcb_ref, dtb_ref, alog_ref, dskip_ref,
                ng_ref, o_ref, xtail_ref, bctail_ref, st_ref):
    @pl.when(pl.program_id(1) == 0)
    def _():
        xtail_ref[...] = jnp.zeros_like(xtail_ref)
        bctail_ref[...] = jnp.zeros_like(bctail_ref)
        st_ref[...] = jnp.zeros_like(st_ref)

    chunk = SSD_CHUNK
    gw = SSD_GROUPS * SSD_STATE
    d_inner = x_ref.shape[2]
    group_width = d_inner // SSD_GROUPS
    pairs = group_width // LANES
    xs = _silu(_causal_conv(x_ref[0], xtail_ref, xw_ref, xb_ref))
    bc = _silu(_causal_conv(bc_ref[0], bctail_ref, bcw_ref, bcb_ref))
    bm = bc[:, :gw]
    cm = bc[:, gw:]

    dt = _softplus(dt_ref[0] + dtb_ref[...])
    da = dt * (-jnp.exp(alog_ref[...]))
    row = lax.broadcasted_iota(jnp.int32, (chunk, LANES), 0)
    col = lax.broadcasted_iota(jnp.int32, (chunk, LANES), 1)
    cs = da
    step = 1
    while step < chunk:
        cs = cs + jnp.where(row >= step, pltpu.roll(cs, step, axis=0), 0.0)
        step *= 2
    cs_t = cs.T
    dt_t = dt.T
    to_end_t = jnp.exp(cs_t[:, chunk - 1:chunk] - cs_t) * dt_t
    from_start = jnp.exp(cs)
    causal = row >= col
    left = col < SSD_HEAD_DIM

    ys = []
    for g in range(SSD_GROUPS):
        gs = slice(g * SSD_STATE, (g + 1) * SSD_STATE)
        b_g, c_g = bm[:, gs], cm[:, gs]
        cb = _dot_nt(c_g.astype(BF16), b_g.astype(BF16))
        b_t = b_g.T
        for pr in range(pairs):
            ps = slice(g * group_width + pr * LANES, g * group_width + (pr + 1) * LANES)
            x_pair = xs[:, ps]
            st_pair = st_ref[g, :, pr * LANES:(pr + 1) * LANES]
            rhs = jnp.concatenate([x_pair, st_pair], axis=0).astype(BF16)
            x_pair_b = x_pair.astype(BF16)
            y_half, st_half = [], []
            for half in range(2):
                h = (g * pairs + pr) * 2 + half
                decay = jnp.where(causal, jnp.exp(cs[:, h:h + 1] - cs_t[h:h + 1, :]), 0.0)
                m_h = decay * cb * dt_t[h:h + 1, :]
                c_h = c_g * from_start[:, h:h + 1]
                lhs = jnp.concatenate([m_h, c_h], axis=1).astype(BF16)
                y_half.append(_dot(lhs, rhs))
                local = _dot((b_t * to_end_t[h:h + 1, :]).astype(BF16), x_pair_b)
                st_half.append(from_start[chunk - 1:chunk, h:h + 1] * st_pair + local)
            ys.append(jnp.where(left, y_half[0], y_half[1]))
            st_ref[g, :, pr * LANES:(pr + 1) * LANES] = jnp.where(left, st_half[0], st_half[1])
    y = jnp.concatenate(ys, axis=1)
    y = y + xs * dskip_ref[...]
    y = y * _silu(z_ref[0])
    outs = []
    for g in range(SSD_GROUPS):
        yg = y[:, g * group_width:(g + 1) * group_width]
        ms = jnp.mean(yg * yg, axis=-1, keepdims=True)
        outs.append(yg * lax.rsqrt(ms + NORM_EPS))
    o_ref[0] = (jnp.concatenate(outs, axis=1) * ng_ref[...]).astype(o_ref.dtype)


def _ssd_branch(proj, dt_proj, dt_col, batch, seq, z_col, xbc_col, d_inner, p):
    bc_width = 2 * SSD_GROUPS * SSD_STATE
    bc_col = xbc_col + d_inner
    heads = d_inner // SSD_HEAD_DIM
    view = proj.reshape(batch, seq, proj.shape[1])
    dt_view = dt_proj.reshape(batch, seq, dt_proj.shape[1])
    pad = lambda v: jnp.pad(v, (0, LANES - heads)).reshape(1, LANES)
    const = lambda shape: pl.BlockSpec(shape, lambda b, c: (0, 0))
    conv_w, conv_b = p["conv_w"], p["conv_b"].reshape(1, d_inner + bc_width)
    out = pl.pallas_call(
        _ssd_kernel,
        grid=(batch, seq // SSD_CHUNK),
        in_specs=[pl.BlockSpec((1, SSD_CHUNK, d_inner), lambda b, c: (b, c, xbc_col // d_inner)),
                  pl.BlockSpec((1, SSD_CHUNK, bc_width), lambda b, c: (b, c, bc_col // bc_width)),
                  pl.BlockSpec((1, SSD_CHUNK, d_inner), lambda b, c: (b, c, z_col // d_inner)),
                  pl.BlockSpec((1, SSD_CHUNK, LANES), lambda b, c: (b, c, dt_col // LANES)),
                  const((CONV_WIDTH, d_inner)), const((1, d_inner)),
                  const((CONV_WIDTH, bc_width)), const((1, bc_width)),
                  const((1, LANES)), const((1, LANES)), const((1, d_inner)), const((1, d_inner))],
        out_specs=pl.BlockSpec((1, SSD_CHUNK, d_inner), lambda b, c: (b, c, 0)),
        out_shape=jax.ShapeDtypeStruct((batch, seq, d_inner), BF16),
        scratch_shapes=[pltpu.VMEM((SUBLANES, d_inner), F32), pltpu.VMEM((SUBLANES, bc_width), F32),
                        pltpu.VMEM((SSD_GROUPS, SSD_STATE, d_inner // SSD_GROUPS), F32)],
        compiler_params=_params("parallel", "arbitrary"),
    )(view, view, view, dt_view, conv_w[:, :d_inner], conv_b[:, :d_inner], conv_w[:, d_inner:],
      conv_b[:, d_inner:], pad(p["dt_bias"]), pad(p["a_log"]),
      jnp.repeat(p["d_skip"], SSD_HEAD_DIM).reshape(1, d_inner), p["norm"].reshape(1, d_inner))
    return out.reshape(batch * seq, d_inner)


def _merge_kernel(gl_ref, ya_ref, yl_ref, ys_ref, gu0_ref, gu1_ref, gu2_ref, gb0_ref, gb1_ref, gb2_ref,
                  pa_ref, pl_ref, ps_ref, o_ref):
    g_low = gl_ref[...].astype(BF16)

    def gated(gu_ref, gb_ref, y_ref, p_ref):
        return jax.nn.sigmoid(_dot(g_low, gu_ref[...]) + gb_ref[...]) * _dot(y_ref[...], p_ref[...])

    y = (gated(gu0_ref, gb0_ref, ya_ref, pa_ref) + gated(gu1_ref, gb1_ref, yl_ref, pl_ref)
         + gated(gu2_ref, gb2_ref, ys_ref, ps_ref))
    o_ref[...] = y.astype(o_ref.dtype)


def _gated_merge(g_low, y_attn, y_lru, y_ssd, gate_up, gate_b, p_attn, p_lru, p_ssd, layer, tm=1024, tn=512):
    m = y_attn.shape[0]
    d = p_attn.shape[2]
    tm, tn = min(tm, m), min(tn, d)
    nd = d // tn
    rows = lambda width: pl.BlockSpec((tm, width), lambda i, j: (i, 0))
    gate_w = lambda br: _weight_tile(gate_up, layer, tn, lambda j: br * nd + j)
    gate_bias = lambda br: pl.BlockSpec((1, tn), lambda i, j: (0, br * nd + j))
    gb = gate_b.reshape(1, 3 * d)
    return pl.pallas_call(
        _merge_kernel,
        grid=(m // tm, nd),
        in_specs=[rows(GATE_RANK), rows(y_attn.shape[1]), rows(y_lru.shape[1]),
                  rows(y_ssd.shape[1]),
                  gate_w(0), gate_w(1), gate_w(2), gate_bias(0), gate_bias(1), gate_bias(2),
                  _weight_tile(p_attn, layer, tn), _weight_tile(p_lru, layer, tn),
                  _weight_tile(p_ssd, layer, tn)],
        out_specs=pl.BlockSpec((tm, tn), lambda i, j: (i, j)),
        out_shape=jax.ShapeDtypeStruct((m, d), BF16),
        compiler_params=_params("parallel", "parallel"),
    )(g_low, y_attn, y_lru, y_ssd, gate_up, gate_up, gate_up, gb, gb, gb, p_attn, p_lru, p_ssd)


def _rms_normalize(x, g):
    ms = jnp.mean(x * x, axis=-1, keepdims=True)
    return x * lax.rsqrt(ms + NORM_EPS) * g


def _xattn_kernel(h_ref, g_ref, wq_ref, kv_ref, wo_ref, gn_ref, o_ref, on_ref):
    h = h_ref[0]
    hn = _rms_normalize(h, g_ref[...]).astype(BF16)
    q = _dot(hn, wq_ref[...]).astype(BF16)
    width = XATTN_HEADS * XATTN_HEAD_DIM
    scale = XATTN_HEAD_DIM ** -0.5
    outs = []
    for hh in range(XATTN_HEADS):
        sl = slice(hh * XATTN_HEAD_DIM, (hh + 1) * XATTN_HEAD_DIM)
        vsl = slice(width + hh * XATTN_HEAD_DIM, width + (hh + 1) * XATTN_HEAD_DIM)
        s = _dot_nt(q[:, sl], kv_ref[0, :, sl]) * scale
        p = jnp.exp(s - jnp.max(s, axis=1, keepdims=True))
        denom = jnp.sum(p, axis=1, keepdims=True)
        outs.append(_dot(p.astype(BF16), kv_ref[0, :, vsl]) / denom)
    o = jnp.concatenate(outs, axis=1).astype(BF16)
    h_new = h + _dot(o, wo_ref[...])
    o_ref[0] = h_new
    on_ref[0] = _rms_normalize(h_new, gn_ref[...]).astype(on_ref.dtype)


def _memory_cross_attention(h, batch, seq, norm_g, w_q, kv, w_o, layer, next_gain, rows=256):
    d = h.shape[1]
    rows = min(rows, seq)
    width = XATTN_HEADS * XATTN_HEAD_DIM
    mem_len = kv.shape[0] // batch
    tile = pl.BlockSpec((1, rows, d), lambda b, i: (b, i, 0))
    gain = pl.BlockSpec((1, d), lambda b, i: (0, 0))
    out, normed = pl.pallas_call(
        _xattn_kernel,
        grid=(batch, seq // rows),
        in_specs=[tile, gain,
                  pl.BlockSpec((None, d, width), lambda b, i: (layer, 0, 0)),
                  pl.BlockSpec((1, mem_len, 2 * width), lambda b, i: (b, 0, 0)),
                  pl.BlockSpec((None, width, d), lambda b, i: (layer, 0, 0)),
                  gain],
        out_specs=[tile, tile],
        out_shape=[jax.ShapeDtypeStruct((batch, seq, d), F32), jax.ShapeDtypeStruct((batch, seq, d), BF16)],
        compiler_params=_params("parallel", "parallel"),
    )(h.reshape(batch, seq, d), norm_g.reshape(1, d), w_q, kv.reshape(batch, mem_len, 2 * width), w_o,
      next_gain.reshape(1, d))
    return out.reshape(batch * seq, d), normed.reshape(batch * seq, d)


def _layer(h, hg, ssq, mem, w, p, layer, next_gain, batch, seq):
    lru_width = p["lru_conv_w"].shape[1]
    d_inner = p["ssd_norm"].shape[0]

    hidden = _mm_swiglu(hg, w["ffn1_w_gu"], layer, ssq)
    h, ug, ussq = _mm_residual(hidden, w["ffn1_w_down"], layer, h, 0.5, p["mix_norm"])

    qkv = _mm(ug, w["in_qkv"], layer, F32, ussq, tn=768)
    proj = _mm(ug, w["in_branches"], layer, F32, ussq)
    small = _mm(ug, w["in_small"], layer, F32, ussq)

    y_attn = _dilated_attention(qkv, batch, seq)
    y_lru = _lru_branch(proj, batch, seq, 0, lru_width, lru_width,
                        dict(conv_w=p["lru_conv_w"], conv_b=p["lru_conv_b"], w_a=w["lru_w_a"], b_a=p["lru_b_a"],
                             w_x=w["lru_w_x"], b_x=p["lru_b_x"], lam=p["lru_lambda"]), layer)
    y_ssd = _ssd_branch(proj, small, GATE_RANK, batch, seq, 2 * lru_width, 2 * lru_width + d_inner, d_inner,
                        dict(conv_w=p["ssd_conv_w"], conv_b=p["ssd_conv_b"], dt_bias=p["ssd_dt_bias"],
                             a_log=p["ssd_a_log"], d_skip=p["ssd_d"], norm=p["ssd_norm"]))
    merged = _gated_merge(small, y_attn, y_lru, y_ssd, w["gate_up"], p["gate_b"],
                          w["w_br_attn"], w["w_br_lru"], w["w_br_ssd"], layer)
    h = _mm_residual(merged, w["w_out"], layer, h, 1.0)

    memn = _rmsnorm(mem, p["mem_norm"], BF16)
    kv = _mm(memn, w["xattn_w_kv"], layer, BF16)
    h, hn = _memory_cross_attention(h, batch, seq, p["xattn_norm"], w["xattn_w_q"], kv, w["xattn_w_o"], layer,
                                    p["ffn2_norm"])

    hidden = _mm_swiglu(hn, w["ffn2_w_gu"], layer)
    if next_gain is None:
        return _mm_residual(hidden, w["ffn2_w_down"], layer, h, 0.5), None, None
    return _mm_residual(hidden, w["ffn2_w_down"], layer, h, 0.5, next_gain)


def kernel(x, mem, ffn1_norm, ffn1_w_gu, ffn1_w_down, mix_norm, w_in, lru_conv_w, lru_conv_b, lru_w_a, lru_b_a, lru_w_x, lru_b_x, lru_lambda, ssd_conv_w, ssd_conv_b, ssd_dt_bias, ssd_a_log, ssd_d, ssd_norm, gate_up, gate_b, w_br_attn, w_br_lru, w_br_ssd, w_out, xattn_norm, mem_norm, xattn_w_q, xattn_w_kv, xattn_w_o, ffn2_norm, ffn2_w_gu, ffn2_w_down, final_norm):
    batch, seq, d = x.shape
    depth = w_in.shape[0]
    heads = ssd_dt_bias.shape[1]
    c_lru = 3 * ATT_WIDTH
    c_dt = c_lru + 2 * lru_conv_w.shape[2] + ssd_norm.shape[1] + ssd_conv_w.shape[2]
    c_gate = c_dt + heads
    in_small = jnp.concatenate([w_in[:, :, c_gate:], w_in[:, :, c_dt:c_gate]], axis=2)
    in_small = jnp.pad(in_small, ((0, 0), (0, 0), (0, LANES - heads)))
    weights = dict(
        ffn1_w_gu=ffn1_w_gu, ffn1_w_down=ffn1_w_down, in_qkv=w_in[:, :, :c_lru],
        in_branches=w_in[:, :, c_lru:c_dt], in_small=in_small, lru_w_a=lru_w_a, lru_w_x=lru_w_x,
        gate_up=gate_up, w_br_attn=w_br_attn, w_br_lru=w_br_lru, w_br_ssd=w_br_ssd, w_out=w_out,
        xattn_w_q=xattn_w_q, xattn_w_kv=xattn_w_kv, xattn_w_o=xattn_w_o, ffn2_w_gu=ffn2_w_gu,
        ffn2_w_down=ffn2_w_down)
    weights = {name: value.astype(BF16) for name, value in weights.items()}
    small = dict(
        mix_norm=mix_norm, lru_conv_w=lru_conv_w, lru_conv_b=lru_conv_b, lru_b_a=lru_b_a, lru_b_x=lru_b_x,
        lru_lambda=lru_lambda, ssd_conv_w=ssd_conv_w, ssd_conv_b=ssd_conv_b, ssd_dt_bias=ssd_dt_bias,
        ssd_a_log=ssd_a_log, ssd_d=ssd_d, ssd_norm=ssd_norm, gate_b=gate_b, xattn_norm=xattn_norm,
        mem_norm=mem_norm, ffn2_norm=ffn2_norm)
    mem2 = mem.reshape(batch * mem.shape[1], d)
    h = x.reshape(batch * seq, d)
    hg, ssq = _gain_ssq(h, ffn1_norm[0])
    for layer in range(depth):
        p = {name: value[layer] for name, value in small.items()}
        next_gain = ffn1_norm[layer + 1] if layer + 1 < depth else None
        h, hg, ssq = _layer(h, hg, ssq, mem2, weights, p, layer, next_gain, batch, seq)
    return _rmsnorm(h, final_norm, F32).reshape(batch, seq, d)
```

```python
import functools

import jax
import jax.numpy as jnp
from jax import lax
from jax.experimental import pallas as pl
from jax.experimental.pallas import tpu as pltpu

F32 = jnp.float32
BF16 = jnp.bfloat16

HEAD_DIM = 128
ATT_PATTERNS = ((128, 1), (512, 4), (2048, 16))
ATT_GROUPS = 3
ATT_GROUP_HEADS = 4
ATT_HEADS = ATT_GROUPS * ATT_GROUP_HEADS
ATT_GROUP_WIDTH = ATT_GROUP_HEADS * HEAD_DIM
ATT_WIDTH = ATT_HEADS * HEAD_DIM
BAND = 128
SEGMENT = 2048
LRU_BLOCK = 128
LRU_C = 8.0
CONV_WIDTH = 4
SSD_HEAD_DIM = 64
SSD_GROUPS = 4
SSD_STATE = 128
SSD_CHUNK = 128
GATE_RANK = 512
XATTN_HEADS = 4
XATTN_HEAD_DIM = 128
NORM_EPS = 1e-6

LANES = 128
SUBLANES = 8
VMEM_LIMIT_BYTES = 56 * 1024 * 1024


def _params(*semantics):
    return pltpu.CompilerParams(dimension_semantics=semantics, vmem_limit_bytes=VMEM_LIMIT_BYTES)


def _softplus(x):
    return jnp.maximum(x, 0.0) + jnp.log1p(jnp.exp(-jnp.abs(x)))


def _silu(x):
    return x * jax.nn.sigmoid(x)


def _dot(a, b):
    return jnp.dot(a, b, preferred_element_type=F32)


def _dot_nt(a, b):
    return lax.dot_general(a, b, (((1,), (1,)), ((), ())), preferred_element_type=F32)


def _rmsnorm_kernel(x_ref, g_ref, o_ref):
    x = x_ref[...]
    ms = jnp.mean(x * x, axis=-1, keepdims=True)
    o_ref[...] = (x * lax.rsqrt(ms + NORM_EPS) * g_ref[...]).astype(o_ref.dtype)


def _rmsnorm(x, g, out_dtype, rows=256):
    t, d = x.shape
    rows = min(rows, t)
    return pl.pallas_call(
        _rmsnorm_kernel,
        grid=(t // rows,),
        in_specs=[pl.BlockSpec((rows, d), lambda i: (i, 0)),
                  pl.BlockSpec((1, d), lambda i: (0, 0))],
        out_specs=pl.BlockSpec((rows, d), lambda i: (i, 0)),
        out_shape=jax.ShapeDtypeStruct((t, d), out_dtype),
        compiler_params=_params("parallel"),
    )(x, g.reshape(1, d))


def _row_rsqrt(ssq_ref, inv_d):
    return lax.rsqrt(ssq_ref[:, :1] * inv_d + NORM_EPS)


def _mm_kernel(*refs, inv_d, transposed):
    dot = _dot_nt if transposed else _dot
    if inv_d is None:
        x_ref, w_ref, o_ref = refs
        o_ref[...] = dot(x_ref[...], w_ref[...]).astype(o_ref.dtype)
    else:
        x_ref, w_ref, ssq_ref, o_ref = refs
        o_ref[...] = (dot(x_ref[...], w_ref[...]) * _row_rsqrt(ssq_ref, inv_d)).astype(o_ref.dtype)


def _weight_tile(stack, layer, tn, col=lambda j: j):
    return pl.BlockSpec((None, stack.shape[1], tn), lambda i, j: (layer, 0, col(j)))


def _mm(x, w, layer, out_dtype, ssq=None, tm=1024, tn=1024, transposed=False):
    m, k = x.shape
    n = w.shape[1] if transposed else w.shape[2]
    tm, tn = min(tm, m), min(tn, n)
    w_spec = (pl.BlockSpec((None, tn, k), lambda i, j: (layer, j, 0)) if transposed
              else _weight_tile(w, layer, tn))
    in_specs = [pl.BlockSpec((tm, k), lambda i, j: (i, 0)), w_spec]
    args = [x, w]
    if ssq is not None:
        in_specs.append(pl.BlockSpec((tm, LANES), lambda i, j: (i, 0)))
        args.append(ssq)
    return pl.pallas_call(
        functools.partial(_mm_kernel, inv_d=None if ssq is None else 1.0 / k, transposed=transposed),
        grid=(m // tm, n // tn),
        in_specs=in_specs,
        out_specs=pl.BlockSpec((tm, tn), lambda i, j: (i, j)),
        out_shape=jax.ShapeDtypeStruct((m, n), out_dtype),
        compiler_params=_params("parallel", "parallel"),
    )(*args)


def _mm_swiglu_kernel(*refs, inv_d):
    if inv_d is None:
        x_ref, wg_ref, wu_ref, o_ref = refs
        scale = None
    else:
        x_ref, wg_ref, wu_ref, ssq_ref, o_ref = refs
        scale = _row_rsqrt(ssq_ref, inv_d)
    x = x_ref[...]
    gate = _dot(x, wg_ref[...])
    up = _dot(x, wu_ref[...])
    if scale is not None:
        gate, up = gate * scale, up * scale
    o_ref[...] = (_silu(gate) * up).astype(o_ref.dtype)


def _mm_swiglu(x, w_gu, layer, ssq=None, tm=1024, tn=512):
    m, k = x.shape
    f = w_gu.shape[2] // 2
    tm, tn = min(tm, m), min(tn, f)
    nf = f // tn
    in_specs = [pl.BlockSpec((tm, k), lambda i, j: (i, 0)),
                _weight_tile(w_gu, layer, tn),
                _weight_tile(w_gu, layer, tn, lambda j: j + nf)]
    args = [x, w_gu, w_gu]
    if ssq is not None:
        in_specs.append(pl.BlockSpec((tm, LANES), lambda i, j: (i, 0)))
        args.append(ssq)
    return pl.pallas_call(
        functools.partial(_mm_swiglu_kernel, inv_d=None if ssq is None else 1.0 / k),
        grid=(m // tm, nf),
        in_specs=in_specs,
        out_specs=pl.BlockSpec((tm, tn), lambda i, j: (i, j)),
        out_shape=jax.ShapeDtypeStruct((m, f), BF16),
        compiler_params=_params("parallel", "parallel"),
    )(*args)


def _mm_residual_kernel(*refs, scale, emit):
    if emit:
        x_ref, w_ref, r_ref, g_ref, o_ref, hg_ref, ssq_ref = refs
    else:
        x_ref, w_ref, r_ref, o_ref = refs
    h = r_ref[...] + scale * _dot(x_ref[...], w_ref[...])
    o_ref[...] = h
    if emit:
        hg_ref[...] = (h * g_ref[...]).astype(hg_ref.dtype)
        part = jnp.broadcast_to(jnp.sum(h * h, axis=1, keepdims=True), ssq_ref.shape)

        @pl.when(pl.program_id(1) == 0)
        def _():
            ssq_ref[...] = part

        @pl.when(pl.program_id(1) > 0)
        def _():
            ssq_ref[...] += part


def _mm_residual(x, w, layer, res, scale, next_gain=None, tm=1024, tn=512):
    m, k = x.shape
    n = w.shape[2]
    tm, tn = min(tm, m), min(tn, n)
    emit = next_gain is not None
    tile = pl.BlockSpec((tm, tn), lambda i, j: (i, j))
    in_specs = [pl.BlockSpec((tm, k), lambda i, j: (i, 0)), _weight_tile(w, layer, tn), tile]
    args = [x, w, res]
    out_specs, out_shape = tile, jax.ShapeDtypeStruct((m, n), F32)
    if emit:
        in_specs.append(pl.BlockSpec((1, tn), lambda i, j: (0, j)))
        args.append(next_gain.reshape(1, n))
        out_specs = [tile, tile, pl.BlockSpec((tm, LANES), lambda i, j: (i, 0))]
        out_shape = [out_shape, jax.ShapeDtypeStruct((m, n), BF16), jax.ShapeDtypeStruct((m, LANES), F32)]
    return pl.pallas_call(
        functools.partial(_mm_residual_kernel, scale=scale, emit=emit),
        grid=(m // tm, n // tn),
        in_specs=in_specs,
        out_specs=out_specs,
        out_shape=out_shape,
        compiler_params=_params("parallel", "arbitrary"),
    )(*args)


def _gain_ssq_kernel(x_ref, g_ref, hg_ref, ssq_ref):
    x = x_ref[...]
    hg_ref[...] = (x * g_ref[...]).astype(hg_ref.dtype)
    ssq_ref[...] = jnp.broadcast_to(jnp.sum(x * x, axis=1, keepdims=True), ssq_ref.shape)


def _gain_ssq(x, g, rows=256):
    t, d = x.shape
    rows = min(rows, t)
    return pl.pallas_call(
        _gain_ssq_kernel,
        grid=(t // rows,),
        in_specs=[pl.BlockSpec((rows, d), lambda i: (i, 0)), pl.BlockSpec((1, d), lambda i: (0, 0))],
        out_specs=[pl.BlockSpec((rows, d), lambda i: (i, 0)), pl.BlockSpec((rows, LANES), lambda i: (i, 0))],
        out_shape=[jax.ShapeDtypeStruct((t, d), BF16), jax.ShapeDtypeStruct((t, LANES), F32)],
        compiler_params=_params("parallel"),
    )(x, g.reshape(1, d))


def _rows(start, dilation):
    return pl.ds(start, BAND, stride=dilation) if dilation > 1 else pl.ds(start, BAND)


def _attn_kernel(slope_ref, *refs):
    ins = refs[:5 * ATT_GROUPS]
    y_ref, o_scr, l_scr = refs[5 * ATT_GROUPS:]
    first_segment = pl.program_id(1) == 0
    row = lax.broadcasted_iota(jnp.int32, (BAND, BAND), 0)
    col = lax.broadcasted_iota(jnp.int32, (BAND, BAND), 1)
    lag_c = row - col
    lag_p = lag_c + BAND
    valid_c = lag_c >= 0
    valid_p_any = lag_p <= BAND
    valid_p_first = lag_p <= jnp.where(first_segment, -1, BAND)
    scale = HEAD_DIM ** -0.5
    for g, (window, dilation) in enumerate(ATT_PATTERNS):
        q_ref, kp_ref, kc_ref, vp_ref, vc_ref = ins[5 * g:5 * g + 5]
        slope = slope_ref[0, :, g:g + 1] * float(dilation)
        bias_c = slope * lag_c.astype(F32)
        bias_p = slope * lag_p.astype(F32)
        span = BAND * dilation
        alibi_c = jnp.where(valid_c, -bias_c, -jnp.inf)
        mask_any = jnp.concatenate([jnp.where(valid_p_any, -bias_p, -jnp.inf), alibi_c], axis=1)
        mask_first = jnp.concatenate([jnp.where(valid_p_first, -bias_p, -jnp.inf), alibi_c], axis=1)
        for r in range(dilation):
            k_prev = kp_ref[0, _rows(r, dilation), :].astype(BF16)
            v_prev = vp_ref[0, _rows(r, dilation), :].astype(BF16)
            for blk in range(SEGMENT // span):
                rows = _rows(blk * span + r, dilation)
                q = q_ref[0, rows, :].astype(BF16)
                k_own = kc_ref[0, rows, :].astype(BF16)
                v_own = vc_ref[0, rows, :].astype(BF16)
                scores = (_dot_nt(q, jnp.concatenate([k_prev, k_own], axis=0)) * scale
                          + (mask_first if blk == 0 else mask_any))
                m = jnp.max(scores, axis=1, keepdims=True)
                p = jnp.exp(scores - m)
                denom = jnp.sum(p, axis=1, keepdims=True)
                out = _dot(p.astype(BF16), jnp.concatenate([v_prev, v_own], axis=0))
                o_scr[g, rows, :] = out / denom
                l_scr[g, rows, :] = jnp.broadcast_to(m + jnp.log(denom), (BAND, HEAD_DIM))
                k_prev, v_prev = k_own, v_own
    l0, l1, l2 = l_scr[0], l_scr[1], l_scr[2]
    m = jnp.maximum(jnp.maximum(l0, l1), l2)
    e0, e1, e2 = jnp.exp(l0 - m), jnp.exp(l1 - m), jnp.exp(l2 - m)
    y = (e0 * o_scr[0] + e1 * o_scr[1] + e2 * o_scr[2]) / (e0 + e1 + e2)
    y_ref[0] = y.astype(y_ref.dtype)


def _dilated_attention(qkv, batch, seq):
    view = qkv.reshape(batch, seq, 3 * ATT_WIDTH)
    heads = ATT_WIDTH // HEAD_DIM
    slopes = jnp.asarray(
        [[[2.0 ** (-8.0 * (g * ATT_GROUP_HEADS + hh + 1) / ATT_HEADS) if g < ATT_GROUPS else 0.0
           for g in range(LANES)]] for hh in range(ATT_GROUP_HEADS)], F32)
    in_specs = [pl.BlockSpec((1, 1, LANES), lambda b, s, hh: (hh, 0, 0))]
    for g, (window, dilation) in enumerate(ATT_PATTERNS):
        assert window // dilation == BAND and SEGMENT % (BAND * dilation) == 0
        span = BAND * dilation
        per_seg = SEGMENT // span

        def cur(which, g=g):
            return pl.BlockSpec((1, SEGMENT, HEAD_DIM),
                                lambda b, s, hh: (b, s, which * heads + g * ATT_GROUP_HEADS + hh))

        def prev(which, g=g, span=span, per_seg=per_seg):
            return pl.BlockSpec((1, span, HEAD_DIM),
                                lambda b, s, hh: (b, jnp.maximum(s * per_seg - 1, 0),
                                                  which * heads + g * ATT_GROUP_HEADS + hh))

        in_specs += [cur(0), prev(1), cur(1), prev(2), cur(2)]
    out = pl.pallas_call(
        _attn_kernel,
        grid=(batch, seq // SEGMENT, ATT_GROUP_HEADS),
        in_specs=in_specs,
        out_specs=pl.BlockSpec((1, SEGMENT, HEAD_DIM), lambda b, s, hh: (b, s, hh)),
        out_shape=jax.ShapeDtypeStruct((batch, seq, ATT_GROUP_WIDTH), BF16),
        scratch_shapes=[pltpu.VMEM((ATT_GROUPS, SEGMENT, HEAD_DIM), F32),
                        pltpu.VMEM((ATT_GROUPS, SEGMENT, HEAD_DIM), F32)],
        compiler_params=_params("parallel", "parallel", "parallel"),
    )(slopes, *([view] * (5 * ATT_GROUPS)))
    return out.reshape(batch * seq, ATT_GROUP_WIDTH)


def _causal_conv(x, tail_ref, w_ref, b_ref):
    rows = x.shape[0]
    ext = jnp.concatenate([tail_ref[...], x], axis=0)
    y = w_ref[CONV_WIDTH - 1:CONV_WIDTH, :] * x + b_ref[...]
    for back in range(1, CONV_WIDTH):
        shifted = pltpu.roll(ext, back, axis=0)[SUBLANES:SUBLANES + rows]
        y = y + w_ref[CONV_WIDTH - 1 - back:CONV_WIDTH - back, :] * shifted
    tail_ref[...] = x[rows - SUBLANES:rows]
    return y


def _lru_kernel(x_ref, g_ref, cw_ref, cb_ref, wa_ref, ba_ref, wx_ref, bx_ref, lam_ref, o_ref,
                tail_ref, h_ref):
    @pl.when(pl.program_id(2) == 0)
    def _():
        tail_ref[...] = jnp.zeros_like(tail_ref)
        h_ref[...] = jnp.zeros_like(h_ref)

    rows, width = x_ref.shape[1], x_ref.shape[2]
    xc = _causal_conv(x_ref[0], tail_ref, cw_ref, cb_ref)
    xb = xc.astype(BF16)
    pre_a, pre_x = [], []
    for blk in range(width // LRU_BLOCK):
        sl = slice(blk * LRU_BLOCK, (blk + 1) * LRU_BLOCK)
        pre_a.append(_dot(xb[:, sl], wa_ref[blk]))
        pre_x.append(_dot(xb[:, sl], wx_ref[blk]))
    r = jax.nn.sigmoid(jnp.concatenate(pre_a, axis=1) + ba_ref[...])
    i = jax.nn.sigmoid(jnp.concatenate(pre_x, axis=1) + bx_ref[...])
    log_a = -LRU_C * r * _softplus(-lam_ref[...])
    a = jnp.exp(log_a)
    one_minus_a2 = 1.0 - a * a
    b = jnp.where(one_minus_a2 > 0.0, one_minus_a2 * lax.rsqrt(one_minus_a2), 0.0) * (i * xc)
    sub = lax.broadcasted_iota(jnp.int32, (SUBLANES, width), 0)
    steps = [(step, sub >= step) for step in (1, 2, 4)]
    carry = h_ref[...]
    hs = []
    for grp in range(rows // SUBLANES):
        sl = slice(grp * SUBLANES, (grp + 1) * SUBLANES)
        a_grp, b_grp = a[sl], b[sl]
        for step, keep in steps:
            a_prev = pltpu.roll(a_grp, step, axis=0)
            b_prev = pltpu.roll(b_grp, step, axis=0)
            b_grp = jnp.where(keep, a_grp * b_prev + b_grp, b_grp)
            a_grp = jnp.where(keep, a_grp * a_prev, a_grp)
        h_grp = a_grp * carry + b_grp
        hs.append(h_grp)
        carry = h_grp[SUBLANES - 1:SUBLANES]
    h_ref[...] = carry
    h = jnp.concatenate(hs, axis=0)
    o_ref[0] = (jax.nn.gelu(g_ref[0]) * h).astype(o_ref.dtype)


def _lru_branch(proj, batch, seq, x_col, g_col, width, p, layer, rows=512, cols=512):
    rows, cols = min(rows, seq), min(cols, width)
    view = proj.reshape(batch, seq, proj.shape[1])
    nblk = cols // LRU_BLOCK
    vec = lambda: pl.BlockSpec((1, cols), lambda b, c, t: (0, c))
    gate_w = lambda: pl.BlockSpec((None, nblk, LRU_BLOCK, LRU_BLOCK), lambda b, c, t: (layer, c, 0, 0))
    out = pl.pallas_call(
        _lru_kernel,
        grid=(batch, width // cols, seq // rows),
        in_specs=[pl.BlockSpec((1, rows, cols), lambda b, c, t: (b, t, x_col // cols + c)),
                  pl.BlockSpec((1, rows, cols), lambda b, c, t: (b, t, g_col // cols + c)),
                  pl.BlockSpec((CONV_WIDTH, cols), lambda b, c, t: (0, c)),
                  vec(), gate_w(), vec(), gate_w(), vec(), vec()],
        out_specs=pl.BlockSpec((1, rows, cols), lambda b, c, t: (b, t, c)),
        out_shape=jax.ShapeDtypeStruct((batch, seq, width), BF16),
        scratch_shapes=[pltpu.VMEM((SUBLANES, cols), F32), pltpu.VMEM((1, cols), F32)],
        compiler_params=_params("parallel", "parallel", "arbitrary"),
    )(view, view, p["conv_w"], p["conv_b"].reshape(1, width), p["w_a"], p["b_a"].reshape(1, width),
      p["w_x"], p["b_x"].reshape(1, width), p["lam"].reshape(1, width))
    return out.reshape(batch * seq, width)


def _ssd_kernel(x_ref, bc_ref, z_ref, dt_ref, xw_ref, xb_ref, bcw_ref, bcb_ref, dtb_ref, alog_ref, dskip_ref,
                ng_ref, o_ref, xtail_ref, bctail_ref, st_ref):
    @pl.when(pl.program_id(1) == 0)
    def _():
        xtail_ref[...] = jnp.zeros_like(xtail_ref)
        bctail_ref[...] = jnp.zeros_like(bctail_ref)
        st_ref[...] = jnp.zeros_like(st_ref)

    chunk = SSD_CHUNK
    gw = SSD_GROUPS * SSD_STATE
    d_inner = x_ref.shape[2]
    group_width = d_inner // SSD_GROUPS
    pairs = group_width // LANES
    xs = _silu(_causal_conv(x_ref[0], xtail_ref, xw_ref, xb_ref))
    bc = _silu(_causal_conv(bc_ref[0], bctail_ref, bcw_ref, bcb_ref))
    bm = bc[:, :gw]
    cm = bc[:, gw:]

    dt = _softplus(dt_ref[0] + dtb_ref[...])
    da = dt * (-jnp.exp(alog_ref[...]))
    row = lax.broadcasted_iota(jnp.int32, (chunk, LANES), 0)
    col = lax.broadcasted_iota(jnp.int32, (chunk, LANES), 1)
    cs = da
    step = 1
    while step < chunk:
        cs = cs + jnp.where(row >= step, pltpu.roll(cs, step, axis=0), 0.0)
        step *= 2
    cs_t = cs.T
    dt_t = dt.T
    to_end_t = jnp.exp(cs_t[:, chunk - 1:chunk] - cs_t) * dt_t
    from_start = jnp.exp(cs)
    causal = row >= col
    left = col < SSD_HEAD_DIM

    ys = []
    for g in range(SSD_GROUPS):
        gs = slice(g * SSD_STATE, (g + 1) * SSD_STATE)
        b_g, c_g = bm[:, gs], cm[:, gs]
        cb = _dot_nt(c_g.astype(BF16), b_g.astype(BF16))
        b_t = b_g.T
        for pr in range(pairs):
            ps = slice(g * group_width + pr * LANES, g * group_width + (pr + 1) * LANES)
            x_pair = xs[:, ps]
            st_pair = st_ref[g, :, pr * LANES:(pr + 1) * LANES]
            rhs = jnp.concatenate([x_pair, st_pair], axis=0).astype(BF16)
            x_pair_b = x_pair.astype(BF16)
            y_half, st_half = [], []
            for half in range(2):
                h = (g * pairs + pr) * 2 + half
                decay = jnp.where(causal, jnp.exp(cs[:, h:h + 1] - cs_t[h:h + 1, :]), 0.0)
                m_h = decay * cb * dt_t[h:h + 1, :]
                c_h = c_g * from_start[:, h:h + 1]
                lhs = jnp.concatenate([m_h, c_h], axis=1).astype(BF16)
                y_half.append(_dot(lhs, rhs))
                local = _dot((b_t * to_end_t[h:h + 1, :]).astype(BF16), x_pair_b)
                st_half.append(from_start[chunk - 1:chunk, h:h + 1] * st_pair + local)
            ys.append(jnp.where(left, y_half[0], y_half[1]))
            st_ref[g, :, pr * LANES:(pr + 1) * LANES] = jnp.where(left, st_half[0], st_half[1])
    y = jnp.concatenate(ys, axis=1)
    y = y + xs * dskip_ref[...]
    y = y * _silu(z_ref[0])
    outs = []
    for g in range(SSD_GROUPS):
        yg = y[:, g * group_width:(g + 1) * group_width]
        ms = jnp.mean(yg * yg, axis=-1, keepdims=True)
        outs.append(yg * lax.rsqrt(ms + NORM_EPS))
    o_ref[0] = (jnp.concatenate(outs, axis=1) * ng_ref[...]).astype(o_ref.dtype)


def _ssd_branch(proj, dt_proj, dt_col, batch, seq, z_col, xbc_col, d_inner, p):
    bc_width = 2 * SSD_GROUPS * SSD_STATE
    bc_col = xbc_col + d_inner
    heads = d_inner // SSD_HEAD_DIM
    view = proj.reshape(batch, seq, proj.shape[1])
    dt_view = dt_proj.reshape(batch, seq, dt_proj.shape[1])
    pad = lambda v: jnp.pad(v, (0, LANES - heads)).reshape(1, LANES)
    const = lambda shape: pl.BlockSpec(shape, lambda b, c: (0, 0))
    conv_w, conv_b = p["conv_w"], p["conv_b"].reshape(1, d_inner + bc_width)
    out = pl.pallas_call(
        _ssd_kernel,
        grid=(batch, seq // SSD_CHUNK),
        in_specs=[pl.BlockSpec((1, SSD_CHUNK, d_inner), lambda b, c: (b, c, xbc_col // d_inner)),
                  pl.BlockSpec((1, SSD_CHUNK, bc_width), lambda b, c: (b, c, bc_col // bc_width)),
                  pl.BlockSpec((1, SSD_CHUNK, d_inner), lambda b, c: (b, c, z_col // d_inner)),
                  pl.BlockSpec((1, SSD_CHUNK, LANES), lambda b, c: (b, c, dt_col // LANES)),
                  const((CONV_WIDTH, d_inner)), const((1, d_inner)),
                  const((CONV_WIDTH, bc_width)), const((1, bc_width)),
                  const((1, LANES)), const((1, LANES)), const((1, d_inner)), const((1, d_inner))],
        out_specs=pl.BlockSpec((1, SSD_CHUNK, d_inner), lambda b, c: (b, c, 0)),
        out_shape=jax.ShapeDtypeStruct((batch, seq, d_inner), BF16),
        scratch_shapes=[pltpu.VMEM((SUBLANES, d_inner), F32), pltpu.VMEM((SUBLANES, bc_width), F32),
                        pltpu.VMEM((SSD_GROUPS, SSD_STATE, d_inner // SSD_GROUPS), F32)],
        compiler_params=_params("parallel", "arbitrary"),
    )(view, view, view, dt_view, conv_w[:, :d_inner], conv_b[:, :d_inner], conv_w[:, d_inner:],
      conv_b[:, d_inner:], pad(p["dt_bias"]), pad(p["a_log"]),
      jnp.repeat(p["d_skip"], SSD_HEAD_DIM).reshape(1, d_inner), p["norm"].reshape(1, d_inner))
    return out.reshape(batch * seq, d_inner)


def _merge_kernel(gl_ref, ya_ref, yl_ref, ys_ref, gu0_ref, gu1_ref, gu2_ref, gb0_ref, gb1_ref, gb2_ref,
                  pa_ref, pl_ref, ps_ref, o_ref):
    g_low = gl_ref[...].astype(BF16)

    def gated(gu_ref, gb_ref, y_ref, p_ref):
        return jax.nn.sigmoid(_dot(g_low, gu_ref[...]) + gb_ref[...]) * _dot(y_ref[...], p_ref[...])

    y = (gated(gu0_ref, gb0_ref, ya_ref, pa_ref) + gated(gu1_ref, gb1_ref, yl_ref, pl_ref)
         + gated(gu2_ref, gb2_ref, ys_ref, ps_ref))
    o_ref[...] = y.astype(o_ref.dtype)


def _gated_merge(g_low, y_attn, y_lru, y_ssd, gate_up, gate_b, p_attn, p_lru, p_ssd, layer, tm=1024, tn=512):
    m = y_attn.shape[0]
    d = p_attn.shape[2]
    tm, tn = min(tm, m), min(tn, d)
    nd = d // tn
    rows = lambda width: pl.BlockSpec((tm, width), lambda i, j: (i, 0))
    gate_w = lambda br: _weight_tile(gate_up, layer, tn, lambda j: br * nd + j)
    gate_bias = lambda br: pl.BlockSpec((1, tn), lambda i, j: (0, br * nd + j))
    gb = gate_b.reshape(1, 3 * d)
    return pl.pallas_call(
        _merge_kernel,
        grid=(m // tm, nd),
        in_specs=[rows(GATE_RANK), rows(y_attn.shape[1]), rows(y_lru.shape[1]),
                  rows(y_ssd.shape[1]),
                  gate_w(0), gate_w(1), gate_w(2), gate_bias(0), gate_bias(1), gate_bias(2),
                  _weight_tile(p_attn, layer, tn), _weight_tile(p_lru, layer, tn),
                  _weight_tile(p_ssd, layer, tn)],
        out_specs=pl.BlockSpec((tm, tn), lambda i, j: (i, j)),
        out_shape=jax.ShapeDtypeStruct((m, d), BF16),
        compiler_params=_params("parallel", "parallel"),
    )(g_low, y_attn, y_lru, y_ssd, gate_up, gate_up, gate_up, gb, gb, gb, p_attn, p_lru, p_ssd)


def _rms_normalize(x, g):
    ms = jnp.mean(x * x, axis=-1, keepdims=True)
    return x * lax.rsqrt(ms + NORM_EPS) * g


def _xattn_kernel(h_ref, g_ref, wq_ref, kv_ref, wo_ref, gn_ref, o_ref, on_ref):
    h = h_ref[0]
    hn = _rms_normalize(h, g_ref[...]).astype(BF16)
    q = _dot(hn, wq_ref[...]).astype(BF16)
    width = XATTN_HEADS * XATTN_HEAD_DIM
    scale = XATTN_HEAD_DIM ** -0.5
    outs = []
    for hh in range(XATTN_HEADS):
        sl = slice(hh * XATTN_HEAD_DIM, (hh + 1) * XATTN_HEAD_DIM)
        vsl = slice(width + hh * XATTN_HEAD_DIM, width + (hh + 1) * XATTN_HEAD_DIM)
        s = _dot_nt(q[:, sl], kv_ref[0, :, sl]) * scale
        p = jnp.exp(s - jnp.max(s, axis=1, keepdims=True))
        denom = jnp.sum(p, axis=1, keepdims=True)
        outs.append(_dot(p.astype(BF16), kv_ref[0, :, vsl]) / denom)
    o = jnp.concatenate(outs, axis=1).astype(BF16)
    h_new = h + _dot(o, wo_ref[...])
    o_ref[0] = h_new
    on_ref[0] = _rms_normalize(h_new, gn_ref[...]).astype(on_ref.dtype)


def _memory_cross_attention(h, batch, seq, norm_g, w_q, kv, w_o, layer, next_gain, rows=256):
    d = h.shape[1]
    rows = min(rows, seq)
    width = XATTN_HEADS * XATTN_HEAD_DIM
    mem_len = kv.shape[0] // batch
    tile = pl.BlockSpec((1, rows, d), lambda b, i: (b, i, 0))
    gain = pl.BlockSpec((1, d), lambda b, i: (0, 0))
    out, normed = pl.pallas_call(
        _xattn_kernel,
        grid=(batch, seq // rows),
        in_specs=[tile, gain,
                  pl.BlockSpec((None, d, width), lambda b, i: (layer, 0, 0)),
                  pl.BlockSpec((1, mem_len, 2 * width), lambda b, i: (b, 0, 0)),
                  pl.BlockSpec((None, width, d), lambda b, i: (layer, 0, 0)),
                  gain],
        out_specs=[tile, tile],
        out_shape=[jax.ShapeDtypeStruct((batch, seq, d), F32), jax.ShapeDtypeStruct((batch, seq, d), BF16)],
        compiler_params=_params("parallel", "parallel"),
    )(h.reshape(batch, seq, d), norm_g.reshape(1, d), w_q, kv.reshape(batch, mem_len, 2 * width), w_o,
      next_gain.reshape(1, d))
    return out.reshape(batch * seq, d), normed.reshape(batch * seq, d)


def _layer(h, hg, ssq, mem, w, p, layer, next_gain, batch, seq):
    lru_width = p["lru_conv_w"].shape[1]
    d_inner = p["ssd_norm"].shape[0]

    hidden = _mm_swiglu(hg, w["ffn1_w_gu"], layer, ssq)
    h, ug, ussq = _mm_residual(hidden, w["ffn1_w_down"], layer, h, 0.5, p["mix_norm"])

    qkv = _mm(ug, w["in_qkv"], layer, F32, ussq, tn=768, transposed=True)
    proj = _mm(ug, w["in_branches"], layer, F32, ussq, transposed=True)
    small = _mm(ug, w["in_small"], layer, F32, ussq)

    y_attn = _dilated_attention(qkv, batch, seq)
    y_lru = _lru_branch(proj, batch, seq, 0, lru_width, lru_width,
                        dict(conv_w=p["lru_conv_w"], conv_b=p["lru_conv_b"], w_a=w["lru_w_a"], b_a=p["lru_b_a"],
                             w_x=w["lru_w_x"], b_x=p["lru_b_x"], lam=p["lru_lambda"]), layer)
    y_ssd = _ssd_branch(proj, small, GATE_RANK, batch, seq, 2 * lru_width, 2 * lru_width + d_inner, d_inner,
                        dict(conv_w=p["ssd_conv_w"], conv_b=p["ssd_conv_b"], dt_bias=p["ssd_dt_bias"],
                             a_log=p["ssd_a_log"], d_skip=p["ssd_d"], norm=p["ssd_norm"]))
    merged = _gated_merge(small, y_attn, y_lru, y_ssd, w["gate_up"], p["gate_b"],
                          w["w_br_attn"], w["w_br_lru"], w["w_br_ssd"], layer)
    h = _mm_residual(merged, w["w_out"], layer, h, 1.0)

    memn = _rmsnorm(mem, p["mem_norm"], BF16)
    kv = _mm(memn, w["xattn_w_kv"], layer, BF16)
    h, hn = _memory_cross_attention(h, batch, seq, p["xattn_norm"], w["xattn_w_q"], kv, w["xattn_w_o"], layer,
                                    p["ffn2_norm"])

    hidden = _mm_swiglu(hn, w["ffn2_w_gu"], layer)
    if next_gain is None:
        return _mm_residual(hidden, w["ffn2_w_down"], layer, h, 0.5), None, None
    return _mm_residual(hidden, w["ffn2_w_down"], layer, h, 0.5, next_gain)


def kernel(x, mem, ffn1_norm, ffn1_w_gu, ffn1_w_down, mix_norm, w_in, lru_conv_w, lru_conv_b, lru_w_a, lru_b_a, lru_w_x, lru_b_x, lru_lambda, ssd_conv_w, ssd_conv_b, ssd_dt_bias, ssd_a_log, ssd_d, ssd_norm, gate_up, gate_b, w_br_attn, w_br_lru, w_br_ssd, w_out, xattn_norm, mem_norm, xattn_w_q, xattn_w_kv, xattn_w_o, ffn2_norm, ffn2_w_gu, ffn2_w_down, final_norm):
    batch, seq, d = x.shape
    depth = w_in.shape[0]
    heads = ssd_dt_bias.shape[1]
    c_lru = 3 * ATT_WIDTH
    c_dt = c_lru + 2 * lru_conv_w.shape[2] + ssd_norm.shape[1] + ssd_conv_w.shape[2]
    c_gate = c_dt + heads
    in_small = jnp.concatenate([w_in[:, :, c_gate:], w_in[:, :, c_dt:c_gate]], axis=2)
    in_small = jnp.pad(in_small, ((0, 0), (0, 0), (0, LANES - heads)))
    weights = dict(
        ffn1_w_gu=ffn1_w_gu, ffn1_w_down=ffn1_w_down, in_qkv=jnp.swapaxes(w_in[:, :, :c_lru], 1, 2),
        in_branches=jnp.swapaxes(w_in[:, :, c_lru:c_dt], 1, 2), in_small=in_small, lru_w_a=lru_w_a,
        lru_w_x=lru_w_x,
        gate_up=gate_up, w_br_attn=w_br_attn, w_br_lru=w_br_lru, w_br_ssd=w_br_ssd, w_out=w_out,
        xattn_w_q=xattn_w_q, xattn_w_kv=xattn_w_kv, xattn_w_o=xattn_w_o, ffn2_w_gu=ffn2_w_gu,
        ffn2_w_down=ffn2_w_down)
    weights = {name: value.astype(BF16) for name, value in weights.items()}
    small = dict(
        mix_norm=mix_norm, lru_conv_w=lru_conv_w, lru_conv_b=lru_conv_b, lru_b_a=lru_b_a, lru_b_x=lru_b_x,
        lru_lambda=lru_lambda, ssd_conv_w=ssd_conv_w, ssd_conv_b=ssd_conv_b, ssd_dt_bias=ssd_dt_bias,
        ssd_a_log=ssd_a_log, ssd_d=ssd_d, ssd_norm=ssd_norm, gate_b=gate_b, xattn_norm=xattn_norm,
        mem_norm=mem_norm, ffn2_norm=ffn2_norm)
    mem2 = mem.reshape(batch * mem.shape[1], d)
    h = x.reshape(batch * seq, d)
    hg, ssq = _gain_ssq(h, ffn1_norm[0])
    for layer in range(depth):
        p = {name: value[layer] for name, value in small.items()}
        next_gain = ffn1_norm[layer + 1] if layer + 1 < depth else None
        h, hg, ssq = _layer(h, hg, ssq, mem2, weights, p, layer, next_gain, batch, seq)
    return _rmsnorm(h, final_norm, F32).reshape(batch, seq, d)
```
